```python
import jax, jax.numpy as jnp
from jax import lax
import numpy as np

D_MODEL = 4096
BATCH = 4
SEQ = 4096
DEPTH = 1

HEAD_DIM = 128
ATTN_WIDTH = D_MODEL // 2
N_ATTN_HEADS = ATTN_WIDTH // HEAD_DIM
POOL_WIDTH = D_MODEL - ATTN_WIDTH
POOL_WINDOWS = (2, 4, 8, 16)
N_POOL_GROUPS = len(POOL_WINDOWS)
POOL_GROUP_WIDTH = POOL_WIDTH // N_POOL_GROUPS
IN_WIDTH = 3 * ATTN_WIDTH + POOL_WIDTH
MOBA_BLOCK = 256
MOBA_TOPK = 3
Q_CHUNK = 16
D_FF = 4 * D_MODEL
ROPE_THETA = 10000.0
LN_EPS = 1e-5
DEEPNORM_ALPHA = (2.0 * DEPTH) ** 0.25
DEEPNORM_BETA = (8.0 * DEPTH) ** -0.25

kernel_name = "hymba_moba_pool_deepnorm_layer"


def layer_norm(x, g, b):
    xf = x.astype(jnp.float32)
    mu = jnp.mean(xf, axis=-1, keepdims=True)
    var = jnp.mean(jnp.square(xf - mu), axis=-1, keepdims=True)
    y = (xf - mu) * lax.rsqrt(var + LN_EPS)
    return (y * g.astype(jnp.float32) + b.astype(jnp.float32)).astype(x.dtype)


def rope(t):
    S_, D_ = t.shape[2], t.shape[3]
    inv_freq = 1.0 / (ROPE_THETA ** (jnp.arange(0, D_, 2, dtype=jnp.float32) / D_))
    ang = jnp.arange(S_, dtype=jnp.float32)[:, None] * inv_freq[None, :]
    cos = jnp.cos(ang).astype(t.dtype)
    sin = jnp.sin(ang).astype(t.dtype)
    t1, t2 = t[..., : D_ // 2], t[..., D_ // 2:]
    return jnp.concatenate([t1 * cos - t2 * sin, t1 * sin + t2 * cos], axis=-1)


def moba_attention(q, k, v):
    B_, H_, S_, D_ = q.shape
    nb = -(-S_ // MOBA_BLOCK)
    pad = nb * MOBA_BLOCK - S_
    kp = jnp.pad(k, ((0, 0), (0, 0), (0, pad), (0, 0)))
    vp = jnp.pad(v, ((0, 0), (0, 0), (0, pad), (0, 0)))
    kb = kp.reshape(B_, H_, nb, MOBA_BLOCK, D_)
    vb = vp.reshape(B_, H_, nb, MOBA_BLOCK, D_)
    kmean = jnp.mean(kb.astype(jnp.float32), axis=3).astype(k.dtype)
    topk = min(MOBA_TOPK, nb)
    scale = D_ ** -0.5
    n_chunks = S_ // Q_CHUNK
    qc = q.reshape(B_, H_, n_chunks, Q_CHUNK, D_).transpose(2, 0, 1, 3, 4)
    b_ix = jnp.arange(B_)[:, None, None, None]
    h_ix = jnp.arange(H_)[None, :, None, None]

    def one_chunk(args):
        qi, ci = args
        start = ci * Q_CHUNK
        blk = start // MOBA_BLOCK
        qpos = start + jnp.arange(Q_CHUNK)
        gate = jnp.einsum('bhcd,bhnd->bhcn', qi, kmean).astype(jnp.float32)
        gate = jnp.where(jnp.arange(nb) < blk, gate, -jnp.inf)
        _, sel = lax.top_k(gate, topk)
        valid = jnp.arange(topk) < blk
        k_sel = kb[b_ix, h_ix, sel]
        v_sel = vb[b_ix, h_ix, sel]
        s_past = jnp.einsum('bhcd,bhcnkd->bhcnk', qi, k_sel).astype(jnp.float32) * scale
        s_past = jnp.where(valid[:, None], s_past, -jnp.inf)
        s_past = s_past.reshape(B_, H_, Q_CHUNK, topk * MOBA_BLOCK)
        k_own = lax.dynamic_index_in_dim(kb, blk, axis=2, keepdims=False)
        v_own = lax.dynamic_index_in_dim(vb, blk, axis=2, keepdims=False)
        s_own = jnp.einsum('bhcd,bhkd->bhck', qi, k_own).astype(jnp.float32) * scale
        kpos = blk * MOBA_BLOCK + jnp.arange(MOBA_BLOCK)
        s_own = jnp.where(kpos[None, :] <= qpos[:, None], s_own, -jnp.inf)
        p = jax.nn.softmax(jnp.concatenate([s_own, s_past], axis=-1), axis=-1).astype(v.dtype)
        p_own = p[..., :MOBA_BLOCK]
        p_past = p[..., MOBA_BLOCK:].reshape(B_, H_, Q_CHUNK, topk, MOBA_BLOCK)
        return (jnp.einsum('bhck,bhkd->bhcd', p_own, v_own)
                + jnp.einsum('bhcnk,bhcnkd->bhcd', p_past, v_sel))

    out = lax.map(one_chunk, (qc, jnp.arange(n_chunks)))
    return out.transpose(1, 2, 0, 3, 4).reshape(B_, H_, S_, D_)


def causal_multiscale_pool(u):
    B_, S_ = u.shape[0], u.shape[1]
    ug = u.reshape(B_, S_, N_POOL_GROUPS, POOL_GROUP_WIDTH).astype(jnp.float32)
    c = jnp.cumsum(ug, axis=1)
    pos = jnp.arange(1, S_ + 1, dtype=jnp.float32)
    outs = []
    for g, w in enumerate(POOL_WINDOWS):
        cg = c[:, :, g]
        lagged = jnp.pad(cg, ((0, 0), (w, 0), (0, 0)))[:, :S_]
        count = jnp.minimum(pos, float(w))[None, :, None]
        outs.append((cg - lagged) / count - ug[:, :, g])
    return jnp.stack(outs, axis=2).astype(u.dtype)


def setup_inputs(seed: int = 0) -> dict:
    key = jax.random.key(seed)
    ks = jax.random.split(key, 12)
    f32 = jnp.float32
    x = jax.random.normal(ks[0], (BATCH, SEQ, D_MODEL), f32)
    s_in = D_MODEL ** -0.5
    w_qk = jax.random.normal(ks[1], (DEPTH, D_MODEL, 2 * ATTN_WIDTH), f32) * s_in
    w_vu = jax.random.normal(ks[2], (DEPTH, D_MODEL, ATTN_WIDTH + POOL_WIDTH), f32) * (s_in * DEEPNORM_BETA)
    w_in = jnp.concatenate([w_qk, w_vu], axis=-1)
    w_pool = jax.random.normal(ks[3], (DEPTH, N_POOL_GROUPS, POOL_GROUP_WIDTH, POOL_GROUP_WIDTH), f32) * POOL_GROUP_WIDTH ** -0.5
    pool_scale = 1.0 + 0.1 * jax.random.normal(ks[4], (DEPTH, POOL_WIDTH), f32)
    w_out = jax.random.normal(ks[5], (DEPTH, D_MODEL, D_MODEL), f32) * (s_in * DEEPNORM_BETA)
    ln1_g = 1.0 + 0.02 * jax.random.normal(ks[6], (DEPTH, D_MODEL), f32)
    ln1_b = 0.02 * jax.random.normal(ks[7], (DEPTH, D_MODEL), f32)
    w_ff1 = jax.random.normal(ks[8], (DEPTH, D_MODEL, D_FF), f32) * (s_in * DEEPNORM_BETA)
    w_ff2 = jax.random.normal(ks[9], (DEPTH, D_FF, D_MODEL), f32) * (D_FF ** -0.5 * DEEPNORM_BETA)
    ln2_g = 1.0 + 0.02 * jax.random.normal(ks[10], (DEPTH, D_MODEL), f32)
    ln2_b = 0.02 * jax.random.normal(ks[11], (DEPTH, D_MODEL), f32)
    return {"x": x, "w_in": w_in, "w_pool": w_pool, "pool_scale": pool_scale,
            "w_out": w_out, "ln1_g": ln1_g, "ln1_b": ln1_b,
            "w_ff1": w_ff1, "w_ff2": w_ff2, "ln2_g": ln2_g, "ln2_b": ln2_b}


def reference(x, w_in, w_pool, pool_scale, w_out, ln1_g, ln1_b, w_ff1, w_ff2, ln2_g, ln2_b):
    B_, S_, _ = x.shape
    for l in range(DEPTH):
        proj = jnp.einsum('bsd,de->bse', x, w_in[l])
        q, k, v, u = jnp.split(proj, [ATTN_WIDTH, 2 * ATTN_WIDTH, 3 * ATTN_WIDTH], axis=-1)
        to_heads = lambda t: t.reshape(B_, S_, N_ATTN_HEADS, HEAD_DIM).transpose(0, 2, 1, 3)
        q, k, v = rope(to_heads(q)), rope(to_heads(k)), to_heads(v)
        attn = moba_attention(q, k, v).transpose(0, 2, 1, 3).reshape(B_, S_, ATTN_WIDTH)
        pooled = causal_multiscale_pool(u)
        pool = jnp.einsum('bsgc,gce->bsge', pooled, w_pool[l])
        pool = (pool * pool_scale[l].reshape(N_POOL_GROUPS, POOL_GROUP_WIDTH)).reshape(B_, S_, POOL_WIDTH)
        mix = jnp.einsum('bse,ed->bsd', jnp.concatenate([attn, pool], axis=-1), w_out[l])
        x = layer_norm(DEEPNORM_ALPHA * x + mix, ln1_g[l], ln1_b[l])
        hid = jnp.square(jax.nn.relu(jnp.einsum('bsd,df->bsf', x, w_ff1[l])))
        ff = jnp.einsum('bsf,fd->bsd', hid, w_ff2[l])
        x = layer_norm(DEEPNORM_ALPHA * x + ff, ln2_g[l], ln2_b[l])
    return x
```

```python
import functools

import jax
import jax.numpy as jnp
from jax import lax
from jax.experimental import pallas as pl
from jax.experimental.pallas import tpu as pltpu

HEAD_DIM = 128
POOL_WINDOWS = (2, 4, 8, 16)
MOBA_BLOCK = 256
MOBA_TOPK = 3
ROPE_THETA = 10000.0
LN_EPS = 1e-5
POOL_HALO = 16

V7X_VMEM_LIMIT_BYTES = 60 * 1024 * 1024

F32 = jnp.float32
BF16 = jnp.bfloat16


def _pick(n, pref):
    t = min(pref, n)
    while n % t:
        t //= 2
    return t


def _params(*sem):
    return pltpu.CompilerParams(dimension_semantics=sem, vmem_limit_bytes=V7X_VMEM_LIMIT_BYTES)


def _proj_heads_kernel(x_ref, w_ref, cos_ref, sin_ref, o_ref, *, rope):
    acc = jnp.dot(x_ref[...], w_ref[...], preferred_element_type=F32)
    for hh in range(o_ref.shape[0]):
        t = acc[:, hh * HEAD_DIM:(hh + 1) * HEAD_DIM]
        if rope:
            t = t * cos_ref[...] + pltpu.roll(t, HEAD_DIM // 2, 1) * sin_ref[...]
        o_ref[hh] = t.astype(o_ref.dtype)


def _proj_plain_kernel(x_ref, w_ref, o_ref):
    o_ref[...] = jnp.dot(x_ref[...], w_ref[...], preferred_element_type=F32).astype(o_ref.dtype)


def _project_heads(xb, w, cos, sin, *, batch, seq, col0, ncols, rope):
    T, D = xb.shape
    tm = _pick(seq, 1024)
    tn = _pick(ncols, 1024)
    spt = seq // tm
    hpt = tn // HEAD_DIM
    j0 = col0 // tn
    return pl.pallas_call(
        functools.partial(_proj_heads_kernel, rope=rope),
        grid=(T // tm, ncols // tn),
        in_specs=[
            pl.BlockSpec((tm, D), lambda i, j: (i, 0)),
            pl.BlockSpec((D, tn), lambda i, j: (0, j + j0)),
            pl.BlockSpec((tm, HEAD_DIM), lambda i, j: (i % spt, 0)),
            pl.BlockSpec((tm, HEAD_DIM), lambda i, j: (i % spt, 0)),
        ],
        out_specs=pl.BlockSpec((None, hpt, tm, HEAD_DIM), lambda i, j: (i // spt, j, i % spt, 0)),
        out_shape=jax.ShapeDtypeStruct((batch, ncols // HEAD_DIM, seq, HEAD_DIM), BF16),
        compiler_params=_params("parallel", "arbitrary"),
        name="proj_rope_heads" if rope else "proj_heads",
    )(xb, w, cos, sin)


def _project_plain(xb, w, *, col0, ncols, out_dtype):
    T, D = xb.shape
    tm = _pick(T, 1024)
    tn = _pick(ncols, 1024)
    j0 = col0 // tn
    return pl.pallas_call(
        _proj_plain_kernel,
        grid=(T // tm, ncols // tn),
        in_specs=[
            pl.BlockSpec((tm, D), lambda i, j: (i, 0)),
            pl.BlockSpec((D, tn), lambda i, j: (0, j + j0)),
        ],
        out_specs=pl.BlockSpec((tm, tn), lambda i, j: (i, j)),
        out_shape=jax.ShapeDtypeStruct((T, ncols), out_dtype),
        compiler_params=_params("parallel", "arbitrary"),
        name="proj_plain",
    )(xb, w)


def _moba_kernel(q_ref, k_ref, v_ref, o_ref, vt_ref, kmean_ref, bias_ref, *, scale):
    blk = MOBA_BLOCK
    nb = k_ref.shape[0] // blk
    i = pl.program_id(2)
    nt = (((1,), (1,)), ((), ()))

    @pl.when(i == 0)
    def _per_head_setup():
        for j in range(nb):
            kj = k_ref[j * blk:(j + 1) * blk, :].astype(F32)
            kmean_ref[j:j + 1, :] = jnp.sum(kj, axis=0, keepdims=True) * (1.0 / blk)
            vt_ref[j] = v_ref[j * blk:(j + 1) * blk, :].astype(F32).T.astype(BF16)

    q = q_ref[...]

    gate = lax.dot_general(kmean_ref[...].astype(BF16), q, nt, preferred_element_type=F32)
    row = lax.broadcasted_iota(jnp.int32, gate.shape, 0)
    past = row < i
    gate = jnp.where(past, gate, -jnp.inf)
    rank = jnp.zeros(gate.shape, jnp.int32)
    for jp in range(nb):
        gj = gate[jp:jp + 1, :]
        beats = (gj > gate) | ((gj == gate) & (jp < row))
        rank = rank + beats.astype(jnp.int32)
    bias_ref[...] = jnp.where(past & (rank < MOBA_TOPK), 0.0, -jnp.inf).astype(F32)

    own = pl.multiple_of(i * blk, blk)
    s = lax.dot_general(k_ref[pl.ds(own, blk), :], q, nt, preferred_element_type=F32) * scale
    kpos = lax.broadcasted_iota(jnp.int32, s.shape, 0)
    qpos = lax.broadcasted_iota(jnp.int32, s.shape, 1)
    s = jnp.where(kpos <= qpos, s, -jnp.inf)
    m0 = jnp.max(s, axis=0, keepdims=True)
    p = jnp.exp(s - m0)
    l0 = jnp.sum(p, axis=0, keepdims=True)
    acc0 = jnp.dot(vt_ref[i], p.astype(BF16), preferred_element_type=F32)

    def past_block(j, carry):
        m, l, acc = carry
        kj = k_ref[pl.ds(pl.multiple_of(j * blk, blk), blk), :]
        s = lax.dot_general(kj, q, nt, preferred_element_type=F32) * scale + bias_ref[pl.ds(j, 1), :]
        m_new = jnp.maximum(m, jnp.max(s, axis=0, keepdims=True))
        alpha = jnp.exp(m - m_new)
        p = jnp.exp(s - m_new)
        l = alpha * l + jnp.sum(p, axis=0, keepdims=True)
        acc = alpha * acc + jnp.dot(vt_ref[j], p.astype(BF16), preferred_element_type=F32)
        return m_new, l, acc

    _, l, acc = lax.fori_loop(0, i, past_block, (m0, l0, acc0))
    o_ref[...] = (acc / l).T.astype(o_ref.dtype)


def _moba_attention(qk, v, *, seq, n_heads):
    batch = qk.shape[0]
    blk = MOBA_BLOCK
    nb = seq // blk
    return pl.pallas_call(
        functools.partial(_moba_kernel, scale=HEAD_DIM ** -0.5),
        grid=(batch, n_heads, nb),
        in_specs=[
            pl.BlockSpec((None, None, blk, HEAD_DIM), lambda b, h, i: (b, h, i, 0)),
            pl.BlockSpec((None, None, seq, HEAD_DIM), lambda b, h, i: (b, n_heads + h, 0, 0)),
            pl.BlockSpec((None, None, seq, HEAD_DIM), lambda b, h, i: (b, h, 0, 0)),
        ],
        out_specs=pl.BlockSpec((None, blk, HEAD_DIM), lambda b, h, i: (b, i, h)),
        out_shape=jax.ShapeDtypeStruct((batch, seq, n_heads * HEAD_DIM), BF16),
        scratch_shapes=[
            pltpu.VMEM((nb, HEAD_DIM, blk), BF16),
            pltpu.VMEM((nb, HEAD_DIM), F32),
            pltpu.VMEM((nb, blk), F32),
        ],
        compiler_params=_params("parallel", "parallel", "arbitrary"),
        name="moba_attention",
    )(qk, qk, v)


def _pool_kernel(u_ref, halo_ref, w_ref, scale_ref, o_ref, ext_ref):
    ts = u_ref.shape[0]
    cg = w_ref.shape[1]
    s = pl.program_id(1)
    first = s == 0
    ext_ref[0:POOL_HALO, :] = jnp.where(first, 0.0, halo_ref[...])
    ext_ref[POOL_HALO:, :] = u_ref[...]
    pos1 = lax.broadcasted_iota(jnp.int32, (ts, cg), 0) + (s * ts + 1)
    for g, w in enumerate(POOL_WINDOWS):
        cs = slice(g * cg, (g + 1) * cg)
        ug = u_ref[:, cs]
        tot = ug
        for d in range(1, w):
            tot = tot + ext_ref[POOL_HALO - d:POOL_HALO - d + ts, cs]
        count = jnp.minimum(pos1, w).astype(F32)
        pooled = tot / count - ug
        out = jnp.dot(pooled.astype(BF16), w_ref[g], preferred_element_type=F32) * scale_ref[:, cs]
        o_ref[:, cs] = out.astype(o_ref.dtype)


def _pool_mixer(u, w_pool, pool_scale, *, batch, seq):
    P = u.shape[-1]
    G, cg, _ = w_pool.shape
    ts = _pick(seq, 512)
    hpt = ts // POOL_HALO
    return pl.pallas_call(
        _pool_kernel,
        grid=(batch, seq // ts),
        in_specs=[
            pl.BlockSpec((None, ts, P), lambda b, s: (b, s, 0)),
            pl.BlockSpec((None, POOL_HALO, P), lambda b, s: (b, jnp.maximum(s * hpt - 1, 0), 0)),
            pl.BlockSpec((G, cg, cg), lambda b, s: (0, 0, 0)),
            pl.BlockSpec((1, P), lambda b, s: (0, 0)),
        ],
        out_specs=pl.BlockSpec((None, ts, P), lambda b, s: (b, s, 0)),
        out_shape=jax.ShapeDtypeStruct((batch, seq, P), BF16),
        scratch_shapes=[pltpu.VMEM((POOL_HALO + ts, P), F32)],
        compiler_params=_params("parallel", "parallel"),
        name="pool_mixer",
    )(u, u, w_pool, pool_scale)


def _layer_norm_tiles(tile, nj, tn, g_ref, b_ref, o_ref):
    d = nj * tn
    tot = jnp.sum(tile(0), axis=1, keepdims=True)
    for jj in range(1, nj):
        tot = tot + jnp.sum(tile(jj), axis=1, keepdims=True)
    mu = tot * (1.0 / d)
    sq = jnp.sum(jnp.square(tile(0) - mu), axis=1, keepdims=True)
    for jj in range(1, nj):
        sq = sq + jnp.sum(jnp.square(tile(jj) - mu), axis=1, keepdims=True)
    rstd = lax.rsqrt(sq * (1.0 / d) + LN_EPS)
    for jj in range(nj):
        cs = slice(jj * tn, (jj + 1) * tn)
        o_ref[:, cs] = (tile(jj) - mu) * rstd * g_ref[:, cs] + b_ref[:, cs]


def _outproj_ln_kernel(a_ref, p_ref, w_ref, x_ref, g_ref, b_ref, o_ref, pre_ref, *, alpha):
    j = pl.program_id(1)
    ka = a_ref.shape[1]
    mix = jnp.dot(a_ref[...], w_ref[0:ka, :], preferred_element_type=F32)
    mix = mix + jnp.dot(p_ref[...], w_ref[ka:, :], preferred_element_type=F32)
    pre_ref[j] = alpha * x_ref[...] + mix

    @pl.when(j == pl.num_programs(1) - 1)
    def _():
        nj, _, tn = pre_ref.shape
        _layer_norm_tiles(lambda jj: pre_ref[jj], nj, tn, g_ref, b_ref, o_ref)


def _outproj_ln(attn, pool, w_out, x2, g, b, *, alpha):
    T, D = x2.shape
    ka, kp = attn.shape[1], pool.shape[1]
    tm = _pick(T, 512)
    tn = _pick(D, 512)
    return pl.pallas_call(
        functools.partial(_outproj_ln_kernel, alpha=alpha),
        grid=(T // tm, D // tn),
        in_specs=[
            pl.BlockSpec((tm, ka), lambda i, j: (i, 0)),
            pl.BlockSpec((tm, kp), lambda i, j: (i, 0)),
            pl.BlockSpec((ka + kp, tn), lambda i, j: (0, j)),
            pl.BlockSpec((tm, tn), lambda i, j: (i, j)),
            pl.BlockSpec((1, D), lambda i, j: (0, 0)),
            pl.BlockSpec((1, D), lambda i, j: (0, 0)),
        ],
        out_specs=pl.BlockSpec((tm, D), lambda i, j: (i, 0)),
        out_shape=jax.ShapeDtypeStruct((T, D), F32),
        scratch_shapes=[pltpu.VMEM((D // tn, tm, tn), F32)],
        compiler_params=_params("parallel", "arbitrary"),
        name="outproj_ln1",
    )(attn, pool, w_out, x2, g, b)


def _ffn_ln_kernel(h_ref, w1_ref, w2_ref, g_ref, b_ref, o_ref, hb_ref, *, alpha, tn):
    f = pl.program_id(1)
    nj = o_ref.shape[1] // tn

    @pl.when(f == 0)
    def _():
        hb_ref[...] = h_ref[...].astype(BF16)
        o_ref[...] = jnp.zeros_like(o_ref)

    hid = jnp.dot(hb_ref[...], w1_ref[...], preferred_element_type=F32)
    hid = jnp.square(jnp.maximum(hid, 0.0)).astype(BF16)
    for jj in range(nj):
        cs = slice(jj * tn, (jj + 1) * tn)
        o_ref[:, cs] += jnp.dot(hid, w2_ref[:, cs], preferred_element_type=F32)

    @pl.when(f == pl.num_programs(1) - 1)
    def _():
        def pre(jj):
            cs = slice(jj * tn, (jj + 1) * tn)
            return alpha * h_ref[:, cs] + o_ref[:, cs]
        _layer_norm_tiles(pre, nj, tn, g_ref, b_ref, o_ref)


def _ffn_ln(h, w1, w2, g, b, *, alpha):
    T, D = h.shape
    F = w1.shape[1]
    tm = _pick(T, 512)
    tf = _pick(F, 512)
    tn = _pick(D, 512)
    return pl.pallas_call(
        functools.partial(_ffn_ln_kernel, alpha=alpha, tn=tn),
        grid=(T // tm, F // tf),
        in_specs=[
            pl.BlockSpec((tm, D), lambda i, f: (i, 0), pipeline_mode=pl.Buffered(1)),
            pl.BlockSpec((D, tf), lambda i, f: (0, f)),
            pl.BlockSpec((tf, D), lambda i, f: (f, 0)),
            pl.BlockSpec((1, D), lambda i, f: (0, 0)),
            pl.BlockSpec((1, D), lambda i, f: (0, 0)),
        ],
        out_specs=pl.BlockSpec((tm, D), lambda i, f: (i, 0)),
        out_shape=jax.ShapeDtypeStruct((T, D), F32),
        scratch_shapes=[pltpu.VMEM((tm, D), BF16)],
        compiler_params=_params("parallel", "arbitrary"),
        name="ffn_ln2",
    )(h, w1, w2, g, b)


def _rope_tables(seq):
    inv_freq = 1.0 / (ROPE_THETA ** (jnp.arange(0, HEAD_DIM, 2, dtype=F32) / HEAD_DIM))
    ang = jnp.arange(seq, dtype=F32)[:, None] * inv_freq[None, :]
    cos, sin = jnp.cos(ang), jnp.sin(ang)
    return jnp.concatenate([cos, cos], axis=-1), jnp.concatenate([-sin, sin], axis=-1)


def kernel(x, w_in, w_pool, pool_scale, w_out, ln1_g, ln1_b, w_ff1, w_ff2, ln2_g, ln2_b):
    B, S, D = x.shape
    depth = w_in.shape[0]
    attn_w = D // 2
    n_heads = attn_w // HEAD_DIM
    pool_w = D - attn_w
    alpha = (2.0 * depth) ** 0.25
    cos, sin = _rope_tables(S)

    x2 = x.reshape(B * S, D)
    for l in range(depth):
        xb = x2.astype(BF16)
        w_in_b = w_in[l].astype(BF16)
        qk = _project_heads(xb, w_in_b, cos, sin, batch=B, seq=S, col0=0, ncols=2 * attn_w, rope=True)
        v = _project_heads(xb, w_in_b, cos, sin, batch=B, seq=S, col0=2 * attn_w, ncols=attn_w, rope=False)
        u = _project_plain(xb, w_in_b, col0=3 * attn_w, ncols=pool_w, out_dtype=F32)

        attn = _moba_attention(qk, v, seq=S, n_heads=n_heads)
        pool = _pool_mixer(u.reshape(B, S, pool_w), w_pool[l].astype(BF16),
                           pool_scale[l].reshape(1, pool_w), batch=B, seq=S)

        h = _outproj_ln(attn.reshape(B * S, attn_w), pool.reshape(B * S, pool_w), w_out[l].astype(BF16),
                        x2, ln1_g[l].reshape(1, D), ln1_b[l].reshape(1, D), alpha=alpha)
        x2 = _ffn_ln(h, w_ff1[l].astype(BF16), w_ff2[l].astype(BF16),
                     ln2_g[l].reshape(1, D), ln2_b[l].reshape(1, D), alpha=alpha)
    return x2.reshape(B, S, D)
```

```python
import functools

import jax
import jax.numpy as jnp
from jax import lax
from jax.experimental import pallas as pl
from jax.experimental.pallas import tpu as pltpu

HEAD_DIM = 128
POOL_WINDOWS = (2, 4, 8, 16)
MOBA_BLOCK = 256
MOBA_TOPK = 3
MOBA_GROUP = 4
MOBA_HEADS_PER_STEP = 4
LOG2_E = 1.4426950408889634
ROPE_THETA = 10000.0
LN_EPS = 1e-5
POOL_HALO = 16

V7X_VMEM_LIMIT_BYTES = 60 * 1024 * 1024

F32 = jnp.float32
BF16 = jnp.bfloat16


def _pick(n, pref):
    t = min(pref, n)
    while n % t:
        t //= 2
    return t


def _params(*sem):
    return pltpu.CompilerParams(dimension_semantics=sem, vmem_limit_bytes=V7X_VMEM_LIMIT_BYTES)


def _proj_heads_kernel(x_ref, w_ref, cos_ref, sin_ref, o_ref, *, rope):
    acc = jnp.dot(x_ref[...], w_ref[...], preferred_element_type=F32)
    for hh in range(o_ref.shape[0]):
        t = acc[:, hh * HEAD_DIM:(hh + 1) * HEAD_DIM]
        if rope:
            t = t * cos_ref[...] + pltpu.roll(t, HEAD_DIM // 2, 1) * sin_ref[...]
        o_ref[hh] = t.astype(o_ref.dtype)


def _proj_plain_kernel(x_ref, w_ref, o_ref):
    o_ref[...] = jnp.dot(x_ref[...], w_ref[...], preferred_element_type=F32).astype(o_ref.dtype)


def _project_heads(xb, w, cos, sin, *, batch, seq, col0, ncols, rope):
    T, D = xb.shape
    tm = _pick(seq, 1024)
    tn = _pick(ncols, 1024)
    spt = seq // tm
    hpt = tn // HEAD_DIM
    j0 = col0 // tn
    return pl.pallas_call(
        functools.partial(_proj_heads_kernel, rope=rope),
        grid=(T // tm, ncols // tn),
        in_specs=[
            pl.BlockSpec((tm, D), lambda i, j: (i, 0)),
            pl.BlockSpec((D, tn), lambda i, j: (0, j + j0)),
            pl.BlockSpec((tm, HEAD_DIM), lambda i, j: (i % spt, 0)),
            pl.BlockSpec((tm, HEAD_DIM), lambda i, j: (i % spt, 0)),
        ],
        out_specs=pl.BlockSpec((None, hpt, tm, HEAD_DIM), lambda i, j: (i // spt, j, i % spt, 0)),
        out_shape=jax.ShapeDtypeStruct((batch, ncols // HEAD_DIM, seq, HEAD_DIM), BF16),
        compiler_params=_params("parallel", "arbitrary"),
        name="proj_rope_heads" if rope else "proj_heads",
    )(xb, w, cos, sin)


def _project_plain(xb, w, *, col0, ncols, out_dtype):
    T, D = xb.shape
    tm = _pick(T, 1024)
    tn = _pick(ncols, 1024)
    j0 = col0 // tn
    return pl.pallas_call(
        _proj_plain_kernel,
        grid=(T // tm, ncols // tn),
        in_specs=[
            pl.BlockSpec((tm, D), lambda i, j: (i, 0)),
            pl.BlockSpec((D, tn), lambda i, j: (0, j + j0)),
        ],
        out_specs=pl.BlockSpec((tm, tn), lambda i, j: (i, j)),
        out_shape=jax.ShapeDtypeStruct((T, ncols), out_dtype),
        compiler_params=_params("parallel", "arbitrary"),
        name="proj_plain",
    )(xb, w)


def _moba_kernel(q_ref, k_ref, v_ref, o_ref, vt_ref, vtg_ref, kmean_ref, bias_ref, *, scale_log2e):
    blk, grp = MOBA_BLOCK, MOBA_GROUP
    hps, seq, _ = k_ref.shape
    nb = seq // blk
    ngrp = nb // grp
    nt = (((1,), (1,)), ((), ()))

    def setup(h):
        for j in range(nb):
            kj = k_ref[h, j * blk:(j + 1) * blk, :].astype(F32)
            kmean_ref[h, j:j + 1, :] = jnp.sum(kj, axis=0, keepdims=True) * (1.0 / blk)
            vt = v_ref[h, j * blk:(j + 1) * blk, :].astype(F32).T.astype(BF16)
            vt_ref[h, j] = vt
            vtg_ref[h, j // grp, :, (j % grp) * blk:(j % grp + 1) * blk] = vt
        gate = lax.dot_general(kmean_ref[h].astype(BF16), q_ref[h], nt, preferred_element_type=F32)
        row = lax.broadcasted_iota(jnp.int32, gate.shape, 0)
        qblk = lax.shift_right_logical(lax.broadcasted_iota(jnp.int32, gate.shape, 1), blk.bit_length() - 1)
        past = row < qblk
        gate = jnp.where(past, gate, -jnp.inf)
        rank = jnp.zeros(gate.shape, jnp.int32)
        for jp in range(nb):
            gj = gate[jp:jp + 1, :]
            beats = (gj > gate) | ((gj == gate) & (jp < row))
            rank = rank + beats.astype(jnp.int32)
        bias = jnp.where(past & (rank < MOBA_TOPK), 0.0, -jnp.inf).astype(F32)
        for i in range(nb):
            for g in range(ngrp):
                bias_ref[h, i, g, 0:grp, :] = bias[g * grp:(g + 1) * grp, i * blk:(i + 1) * blk]

    for h in range(hps):
        setup(h)

    kpos = lax.broadcasted_iota(jnp.int32, (blk, blk), 0)
    qpos = lax.broadcasted_iota(jnp.int32, (blk, blk), 1)
    causal = kpos <= qpos

    def scores(k_rows, q):
        return lax.dot_general(k_rows, q, nt, preferred_element_type=F32) * scale_log2e

    def biased_tiles(s, bias_rows):
        return [s[t * blk:(t + 1) * blk] + bias_rows[t:t + 1, :] for t in range(grp)]

    def tiles_max(tiles):
        m = jnp.max(tiles[0], axis=0, keepdims=True)
        for t in tiles[1:]:
            m = jnp.maximum(m, jnp.max(t, axis=0, keepdims=True))
        return m

    def tiles_sum(tiles):
        tot = jnp.sum(tiles[0], axis=0, keepdims=True)
        for t in tiles[1:]:
            tot = tot + jnp.sum(t, axis=0, keepdims=True)
        return tot

    heads = range(hps)

    def first_group(i, qs):
        q = [q_ref[h, pl.ds(qs, blk), :] for h in heads]
        s_own = [scores(k_ref[h, pl.ds(qs, blk), :], q[h]) for h in heads]
        s_grp = [scores(k_ref[h, 0:grp * blk, :], q[h]) for h in heads]
        ml, p_own, p = [], [], []
        for h in heads:
            own = jnp.where(causal, s_own[h], -jnp.inf)
            tiles = biased_tiles(s_grp[h], bias_ref[h, i, 0, 0:grp, :])
            m = tiles_max([own] + tiles)
            p_own.append(jnp.exp2(own - m))
            p.append([jnp.exp2(t - m) for t in tiles])
            ml.append((m, tiles_sum([p_own[h]] + p[h])))
        acc = [jnp.dot(vt_ref[h, i], p_own[h].astype(BF16), preferred_element_type=F32)
               + jnp.dot(vtg_ref[h, 0], jnp.concatenate(p[h], axis=0).astype(BF16),
                         preferred_element_type=F32) for h in heads]
        return tuple((ml[h][0], ml[h][1], acc[h]) for h in heads)

    def past_group(i, qs, g, state):
        k0 = pl.multiple_of(g * (grp * blk), grp * blk)
        s_grp = [scores(k_ref[h, pl.ds(k0, grp * blk), :], q_ref[h, pl.ds(qs, blk), :]) for h in heads]
        stats, p = [], []
        for h in heads:
            m, l, _ = state[h]
            tiles = biased_tiles(s_grp[h], bias_ref[h, i, g, 0:grp, :])
            m_new = jnp.maximum(m, tiles_max(tiles))
            alpha = jnp.exp2(m - m_new)
            p.append([jnp.exp2(t - m_new) for t in tiles])
            stats.append((m_new, alpha * l + tiles_sum(p[h]), alpha))
        pv = [jnp.dot(vtg_ref[h, g], jnp.concatenate(p[h], axis=0).astype(BF16),
                      preferred_element_type=F32) for h in heads]
        return tuple((stats[h][0], stats[h][1], stats[h][2] * state[h][2] + pv[h]) for h in heads)

    def q_block(i, _):
        qs = pl.multiple_of(i * blk, blk)
        n_groups = lax.shift_right_logical(i + (grp - 1), grp.bit_length() - 1)
        state = lax.fori_loop(1, n_groups, functools.partial(past_group, i, qs), first_group(i, qs))
        for h, (_, l, acc) in enumerate(state):
            o_ref[pl.ds(qs, blk), h * HEAD_DIM:(h + 1) * HEAD_DIM] = (acc / l).T.astype(o_ref.dtype)
        return 0

    lax.fori_loop(0, nb, q_block, 0)


def _moba_attention(qk, v, *, seq, n_heads):
    batch = qk.shape[0]
    blk, grp = MOBA_BLOCK, MOBA_GROUP
    hps = _pick(n_heads, MOBA_HEADS_PER_STEP)
    nb = seq // blk
    assert seq % (blk * grp) == 0 and blk & (blk - 1) == 0 and grp & (grp - 1) == 0
    heads = lambda off: pl.BlockSpec((None, hps, seq, HEAD_DIM), lambda b, h: (b, h + off, 0, 0))
    return pl.pallas_call(
        functools.partial(_moba_kernel, scale_log2e=HEAD_DIM ** -0.5 * LOG2_E),
        grid=(batch, n_heads // hps),
        in_specs=[heads(0), heads(n_heads // hps), heads(0)],
        out_specs=pl.BlockSpec((None, seq, hps * HEAD_DIM), lambda b, h: (b, 0, h)),
        out_shape=jax.ShapeDtypeStruct((batch, seq, n_heads * HEAD_DIM), BF16),
        scratch_shapes=[
            pltpu.VMEM((hps, nb, HEAD_DIM, blk), BF16),
            pltpu.VMEM((hps, nb // grp, HEAD_DIM, grp * blk), BF16),
            pltpu.VMEM((hps, nb, HEAD_DIM), F32),
            pltpu.VMEM((hps, nb, nb // grp, 8, blk), F32),
        ],
        compiler_params=_params("parallel", "parallel"),
        name="moba_attention",
    )(qk, qk, v)


def _pool_kernel(u_ref, halo_ref, w_ref, scale_ref, o_ref, ext_ref):
    ts = u_ref.shape[0]
    cg = w_ref.shape[1]
    s = pl.program_id(1)
    first = s == 0
    ext_ref[0:POOL_HALO, :] = jnp.where(first, 0.0, halo_ref[...])
    ext_ref[POOL_HALO:, :] = u_ref[...]
    pos1 = lax.broadcasted_iota(jnp.int32, (ts, cg), 0) + (s * ts + 1)
    for g, w in enumerate(POOL_WINDOWS):
        cs = slice(g * cg, (g + 1) * cg)
        ug = u_ref[:, cs]
        tot = ug
        for d in range(1, w):
            tot = tot + ext_ref[POOL_HALO - d:POOL_HALO - d + ts, cs]
        count = jnp.minimum(pos1, w).astype(F32)
        pooled = tot / count - ug
        out = jnp.dot(pooled.astype(BF16), w_ref[g], preferred_element_type=F32) * scale_ref[:, cs]
        o_ref[:, cs] = out.astype(o_ref.dtype)


def _pool_mixer(u, w_pool, pool_scale, *, batch, seq):
    P = u.shape[-1]
    G, cg, _ = w_pool.shape
    ts = _pick(seq, 512)
    hpt = ts // POOL_HALO
    return pl.pallas_call(
        _pool_kernel,
        grid=(batch, seq // ts),
        in_specs=[
            pl.BlockSpec((None, ts, P), lambda b, s: (b, s, 0)),
            pl.BlockSpec((None, POOL_HALO, P), lambda b, s: (b, jnp.maximum(s * hpt - 1, 0), 0)),
            pl.BlockSpec((G, cg, cg), lambda b, s: (0, 0, 0)),
            pl.BlockSpec((1, P), lambda b, s: (0, 0)),
        ],
        out_specs=pl.BlockSpec((None, ts, P), lambda b, s: (b, s, 0)),
        out_shape=jax.ShapeDtypeStruct((batch, seq, P), BF16),
        scratch_shapes=[pltpu.VMEM((POOL_HALO + ts, P), F32)],
        compiler_params=_params("parallel", "parallel"),
        name="pool_mixer",
    )(u, u, w_pool, pool_scale)


def _layer_norm_tiles(tile, nj, tn, g_ref, b_ref, o_ref):
    d = nj * tn
    tot = jnp.sum(tile(0), axis=1, keepdims=True)
    for jj in range(1, nj):
        tot = tot + jnp.sum(tile(jj), axis=1, keepdims=True)
    mu = tot * (1.0 / d)
    sq = jnp.sum(jnp.square(tile(0) - mu), axis=1, keepdims=True)
    for jj in range(1, nj):
        sq = sq + jnp.sum(jnp.square(tile(jj) - mu), axis=1, keepdims=True)
    rstd = lax.rsqrt(sq * (1.0 / d) + LN_EPS)
    for jj in range(nj):
        cs = slice(jj * tn, (jj + 1) * tn)
        o_ref[:, cs] = (tile(jj) - mu) * rstd * g_ref[:, cs] + b_ref[:, cs]


def _outproj_ln_kernel(a_ref, p_ref, w_ref, x_ref, g_ref, b_ref, o_ref, pre_ref, *, alpha):
    j = pl.program_id(1)
    ka = a_ref.shape[1]
    mix = jnp.dot(a_ref[...], w_ref[0:ka, :], preferred_element_type=F32)
    mix = mix + jnp.dot(p_ref[...], w_ref[ka:, :], preferred_element_type=F32)
    pre_ref[j] = alpha * x_ref[...] + mix

    @pl.when(j == pl.num_programs(1) - 1)
    def _():
        nj, _, tn = pre_ref.shape
        _layer_norm_tiles(lambda jj: pre_ref[jj], nj, tn, g_ref, b_ref, o_ref)


def _outproj_ln(attn, pool, w_out, x2, g, b, *, alpha):
    T, D = x2.shape
    ka, kp = attn.shape[1], pool.shape[1]
    tm = _pick(T, 512)
    tn = _pick(D, 512)
    return pl.pallas_call(
        functools.partial(_outproj_ln_kernel, alpha=alpha),
        grid=(T // tm, D // tn),
        in_specs=[
            pl.BlockSpec((tm, ka), lambda i, j: (i, 0)),
            pl.BlockSpec((tm, kp), lambda i, j: (i, 0)),
            pl.BlockSpec((ka + kp, tn), lambda i, j: (0, j)),
            pl.BlockSpec((tm, tn), lambda i, j: (i, j)),
            pl.BlockSpec((1, D), lambda i, j: (0, 0)),
            pl.BlockSpec((1, D), lambda i, j: (0, 0)),
        ],
        out_specs=pl.BlockSpec((tm, D), lambda i, j: (i, 0)),
        out_shape=jax.ShapeDtypeStruct((T, D), F32),
        scratch_shapes=[pltpu.VMEM((D // tn, tm, tn), F32)],
        compiler_params=_params("parallel", "arbitrary"),
        name="outproj_ln1",
    )(attn, pool, w_out, x2, g, b)


def _ffn_ln_kernel(h_ref, w1_ref, w2_ref, g_ref, b_ref, o_ref, hb_ref, *, alpha, tn):
    f = pl.program_id(1)
    nj = o_ref.shape[1] // tn

    @pl.when(f == 0)
    def _():
        hb_ref[...] = h_ref[...].astype(BF16)
        o_ref[...] = jnp.zeros_like(o_ref)

    hid = jnp.dot(hb_ref[...], w1_ref[...], preferred_element_type=F32)
    hid = jnp.square(jnp.maximum(hid, 0.0)).astype(BF16)
    for jj in range(nj):
        cs = slice(jj * tn, (jj + 1) * tn)
        o_ref[:, cs] += jnp.dot(hid, w2_ref[:, cs], preferred_element_type=F32)

    @pl.when(f == pl.num_programs(1) - 1)
    def _():
        def pre(jj):
            cs = slice(jj * tn, (jj + 1) * tn)
            return alpha * h_ref[:, cs] + o_ref[:, cs]
        _layer_norm_tiles(pre, nj, tn, g_ref, b_ref, o_ref)


def _ffn_ln(h, w1, w2, g, b, *, alpha):
    T, D = h.shape
    F = w1.shape[1]
    tm = _pick(T, 512)
    tf = _pick(F, 512)
    tn = _pick(D, 512)
    return pl.pallas_call(
        functools.partial(_ffn_ln_kernel, alpha=alpha, tn=tn),
        grid=(T // tm, F // tf),
        in_specs=[
            pl.BlockSpec((tm, D), lambda i, f: (i, 0), pipeline_mode=pl.Buffered(1)),
            pl.BlockSpec((D, tf), lambda i, f: (0, f)),
            pl.BlockSpec((tf, D), lambda i, f: (f, 0)),
            pl.BlockSpec((1, D), lambda i, f: (0, 0)),
            pl.BlockSpec((1, D), lambda i, f: (0, 0)),
        ],
        out_specs=pl.BlockSpec((tm, D), lambda i, f: (i, 0)),
        out_shape=jax.ShapeDtypeStruct((T, D), F32),
        scratch_shapes=[pltpu.VMEM((tm, D), BF16)],
        compiler_params=_params("parallel", "arbitrary"),
        name="ffn_ln2",
    )(h, w1, w2, g, b)


def _rope_tables(seq):
    inv_freq = 1.0 / (ROPE_THETA ** (jnp.arange(0, HEAD_DIM, 2, dtype=F32) / HEAD_DIM))
    ang = jnp.arange(seq, dtype=F32)[:, None] * inv_freq[None, :]
    cos, sin = jnp.cos(ang), jnp.sin(ang)
    return jnp.concatenate([cos, cos], axis=-1), jnp.concatenate([-sin, sin], axis=-1)


def kernel(x, w_in, w_pool, pool_scale, w_out, ln1_g, ln1_b, w_ff1, w_ff2, ln2_g, ln2_b):
    B, S, D = x.shape
    depth = w_in.shape[0]
    attn_w = D // 2
    n_heads = attn_w // HEAD_DIM
    pool_w = D - attn_w
    alpha = (2.0 * depth) ** 0.25
    cos, sin = _rope_tables(S)

    x2 = x.reshape(B * S, D)
    for l in range(depth):
        xb = x2.astype(BF16)
        w_in_b = w_in[l].astype(BF16)
        qk = _project_heads(xb, w_in_b, cos, sin, batch=B, seq=S, col0=0, ncols=2 * attn_w, rope=True)
        v = _project_heads(xb, w_in_b, cos, sin, batch=B, seq=S, col0=2 * attn_w, ncols=attn_w, rope=False)
        u = _project_plain(xb, w_in_b, col0=3 * attn_w, ncols=pool_w, out_dtype=F32)

        attn = _moba_attention(qk, v, seq=S, n_heads=n_heads)
        pool = _pool_mixer(u.reshape(B, S, pool_w), w_pool[l].astype(BF16),
                           pool_scale[l].reshape(1, pool_w), batch=B, seq=S)

        h = _outproj_ln(attn.reshape(B * S, attn_w), pool.reshape(B * S, pool_w), w_out[l].astype(BF16),
                        x2, ln1_g[l].reshape(1, D), ln1_b[l].reshape(1, D), alpha=alpha)
        x2 = _ffn_ln(h, w_ff1[l].astype(BF16), w_ff2[l].astype(BF16),
                     ln2_g[l].reshape(1, D), ln2_b[l].reshape(1, D), alpha=alpha)
    return x2.reshape(B, S, D)
```

```python
import functools

import jax
import jax.numpy as jnp
from jax import lax
from jax.experimental import pallas as pl
from jax.experimental.pallas import tpu as pltpu

HEAD_DIM = 128
POOL_WINDOWS = (2, 4, 8, 16)
MOBA_BLOCK = 256
MOBA_TOPK = 3
MOBA_GROUP = 4
MOBA_HEADS_PER_STEP = 2
LOG2_E = 1.4426950408889634
ROPE_THETA = 10000.0
LN_EPS = 1e-5
POOL_HALO = 16

V7X_VMEM_LIMIT_BYTES = 60 * 1024 * 1024

F32 = jnp.float32
BF16 = jnp.bfloat16


def _pick(n, pref):
    t = min(pref, n)
    while n % t:
        t //= 2
    return t


def _params(*sem):
    return pltpu.CompilerParams(dimension_semantics=sem, vmem_limit_bytes=V7X_VMEM_LIMIT_BYTES)


def _proj_heads_kernel(x_ref, w_ref, cos_ref, sin_ref, ci_ref, o_ref, co_ref, *, rope):
    co_ref[...] = ci_ref[...].astype(co_ref.dtype)
    acc = jnp.dot(x_ref[...], w_ref[...], preferred_element_type=F32)
    for hh in range(o_ref.shape[0]):
        t = acc[:, hh * HEAD_DIM:(hh + 1) * HEAD_DIM]
        if rope:
            t = t * cos_ref[...] + pltpu.roll(t, HEAD_DIM // 2, 1) * sin_ref[...]
        o_ref[hh] = t.astype(o_ref.dtype)


def _proj_plain_kernel(x_ref, w_ref, o_ref):
    o_ref[...] = jnp.dot(x_ref[...], w_ref[...], preferred_element_type=F32).astype(o_ref.dtype)


def _cast_stream(w_f32, grid):
    R, C = w_f32.shape
    steps = grid[0] * grid[1]
    rb = R // steps
    assert R % steps == 0 and rb % 16 == 0, (R, steps)
    spec = pl.BlockSpec((rb, C), lambda i, j: (i * grid[1] + j, 0))
    return spec, jax.ShapeDtypeStruct((R, C), BF16)


def _project_heads(xb, w, cos, sin, w_cast, *, batch, seq, col0, ncols, rope):
    T, D = xb.shape
    tm = _pick(seq, 1024)
    tn = _pick(ncols, 1024)
    spt = seq // tm
    hpt = tn // HEAD_DIM
    j0 = col0 // tn
    grid = (T // tm, ncols // tn)
    cast_spec, cast_shape = _cast_stream(w_cast, grid)
    return pl.pallas_call(
        functools.partial(_proj_heads_kernel, rope=rope),
        grid=grid,
        in_specs=[
            pl.BlockSpec((tm, D), lambda i, j: (i, 0)),
            pl.BlockSpec((D, tn), lambda i, j: (0, j + j0)),
            pl.BlockSpec((tm, HEAD_DIM), lambda i, j: (i % spt, 0)),
            pl.BlockSpec((tm, HEAD_DIM), lambda i, j: (i % spt, 0)),
            cast_spec,
        ],
        out_specs=[
            pl.BlockSpec((None, hpt, tm, HEAD_DIM), lambda i, j: (i // spt, j, i % spt, 0)),
            cast_spec,
        ],
        out_shape=[jax.ShapeDtypeStruct((batch, ncols // HEAD_DIM, seq, HEAD_DIM), BF16), cast_shape],
        compiler_params=_params("parallel", "arbitrary"),
        name="proj_rope_heads" if rope else "proj_heads",
    )(xb, w, cos, sin, w_cast)


def _project_plain(xb, w, *, col0, ncols, out_dtype):
    T, D = xb.shape
    tm = _pick(T, 1024)
    tn = _pick(ncols, 1024)
    j0 = col0 // tn
    return pl.pallas_call(
        _proj_plain_kernel,
        grid=(T // tm, ncols // tn),
        in_specs=[
            pl.BlockSpec((tm, D), lambda i, j: (i, 0)),
            pl.BlockSpec((D, tn), lambda i, j: (0, j + j0)),
        ],
        out_specs=pl.BlockSpec((tm, tn), lambda i, j: (i, j)),
        out_shape=jax.ShapeDtypeStruct((T, ncols), out_dtype),
        compiler_params=_params("parallel", "arbitrary"),
        name="proj_plain",
    )(xb, w)


def _moba_kernel(q_ref, k_ref, v_ref, o_ref, vtg_ref, kmean_ref, bias_ref, *, scale_log2e):
    blk, grp = MOBA_BLOCK, MOBA_GROUP
    quad = grp * blk
    hps, seq, _ = k_ref.shape
    nb = seq // blk
    ngrp = nb // grp
    nt = (((1,), (1,)), ((), ()))

    def setup(h):
        for j in range(nb):
            kj = k_ref[h, j * blk:(j + 1) * blk, :].astype(F32)
            kmean_ref[h, j:j + 1, :] = jnp.sum(kj, axis=0, keepdims=True) * (1.0 / blk)
            vt = v_ref[h, j * blk:(j + 1) * blk, :].astype(F32).T.astype(BF16)
            vtg_ref[h, j // grp, :, (j % grp) * blk:(j % grp + 1) * blk] = vt
        gate = lax.dot_general(kmean_ref[h].astype(BF16), q_ref[h], nt, preferred_element_type=F32)
        row = lax.broadcasted_iota(jnp.int32, gate.shape, 0)
        qblk = lax.shift_right_logical(lax.broadcasted_iota(jnp.int32, gate.shape, 1), blk.bit_length() - 1)
        past = row < qblk
        gate = jnp.where(past, gate, -jnp.inf)
        rank = jnp.zeros(gate.shape, jnp.int32)
        for jp in range(nb):
            gj = gate[jp:jp + 1, :]
            beats = (gj > gate) | ((gj == gate) & (jp < row))
            rank = rank + beats.astype(jnp.int32)
        bias = jnp.where(past & (rank < MOBA_TOPK), 0.0, -jnp.inf).astype(F32)
        for g in range(ngrp):
            for qd in range(g, ngrp):
                bias_ref[h, g, qd, 0:grp, :] = bias[g * grp:(g + 1) * grp, qd * quad:(qd + 1) * quad]

    for h in range(hps):
        setup(h)

    kpos = lax.broadcasted_iota(jnp.int32, (blk, blk), 0)
    qpos = lax.broadcasted_iota(jnp.int32, (blk, blk), 1)
    causal = kpos <= qpos

    def scores(k_rows, q):
        return lax.dot_general(k_rows, q, nt, preferred_element_type=F32) * scale_log2e

    def tiles_max(tiles):
        m = jnp.max(tiles[0], axis=0, keepdims=True)
        for t in tiles[1:]:
            m = jnp.maximum(m, jnp.max(t, axis=0, keepdims=True))
        return m

    def tiles_sum(tiles):
        tot = jnp.sum(tiles[0], axis=0, keepdims=True)
        for t in tiles[1:]:
            tot = tot + jnp.sum(t, axis=0, keepdims=True)
        return tot

    heads = range(hps)

    def diagonal_group(qd, q0):
        s = [[scores(k_ref[h, pl.ds(q0, (r + 1) * blk), :],
                     q_ref[h, pl.ds(pl.multiple_of(q0 + r * blk, blk), blk), :]) for r in range(grp)]
             for h in heads]
        m, l, p = [], [], []
        for h in heads:
            mr, lr, pr = [], [], []
            for r in range(grp):
                brow = bias_ref[h, qd, qd, 0:grp, r * blk:(r + 1) * blk]
                tiles = [s[h][r][t * blk:(t + 1) * blk] + brow[t:t + 1, :] for t in range(r)]
                tiles.append(jnp.where(causal, s[h][r][r * blk:(r + 1) * blk], -jnp.inf))
                mr.append(tiles_max(tiles))
                pt = [jnp.exp2(t - mr[r]) for t in tiles]
                lr.append(tiles_sum(pt))
                pr.append(jnp.concatenate(pt, axis=0).astype(BF16))
            m.append(jnp.concatenate(mr, axis=1))
            l.append(jnp.concatenate(lr, axis=1))
            p.append(pr)
        acc = [jnp.concatenate(
            [jnp.dot(vtg_ref[h, qd, :, 0:(r + 1) * blk], p[h][r], preferred_element_type=F32)
             for r in range(grp)], axis=1) for h in heads]
        return tuple((m[h], l[h], acc[h]) for h in heads)

    def past_group(qd, q0, g, state):
        k0 = pl.multiple_of(g * quad, quad)
        s = [scores(k_ref[h, pl.ds(k0, quad), :], q_ref[h, pl.ds(q0, quad), :]) for h in heads]
        stats, p = [], []
        for h in heads:
            m, l, _ = state[h]
            brow = bias_ref[h, g, qd, 0:grp, :]
            tiles = [s[h][t * blk:(t + 1) * blk] + brow[t:t + 1, :] for t in range(grp)]
            m_new = jnp.maximum(m, tiles_max(tiles))
            alpha = jnp.exp2(m - m_new)
            pt = [jnp.exp2(t - m_new) for t in tiles]
            p.append(jnp.concatenate(pt, axis=0).astype(BF16))
            stats.append((m_new, alpha * l + tiles_sum(pt), alpha))
        pv = [jnp.dot(vtg_ref[h, g], p[h], preferred_element_type=F32) for h in heads]
        return tuple((stats[h][0], stats[h][1], stats[h][2] * state[h][2] + pv[h]) for h in heads)

    def query_quad(qd, _):
        q0 = pl.multiple_of(qd * quad, quad)
        state = lax.fori_loop(0, qd, functools.partial(past_group, qd, q0), diagonal_group(qd, q0))
        for h, (_, l, acc) in enumerate(state):
            o_ref[pl.ds(q0, quad), h * HEAD_DIM:(h + 1) * HEAD_DIM] = (acc / l).T.astype(o_ref.dtype)
        return 0

    lax.fori_loop(0, ngrp, query_quad, 0)


def _moba_attention(qk, v, *, seq, n_heads):
    batch = qk.shape[0]
    blk, grp = MOBA_BLOCK, MOBA_GROUP
    hps = _pick(n_heads, MOBA_HEADS_PER_STEP)
    nb = seq // blk
    assert seq % (blk * grp) == 0 and blk & (blk - 1) == 0 and grp & (grp - 1) == 0
    heads = lambda off: pl.BlockSpec((None, hps, seq, HEAD_DIM), lambda b, h: (b, h + off, 0, 0))
    return pl.pallas_call(
        functools.partial(_moba_kernel, scale_log2e=HEAD_DIM ** -0.5 * LOG2_E),
        grid=(batch, n_heads // hps),
        in_specs=[heads(0), heads(n_heads // hps), heads(0)],
        out_specs=pl.BlockSpec((None, seq, hps * HEAD_DIM), lambda b, h: (b, 0, h)),
        out_shape=jax.ShapeDtypeStruct((batch, seq, n_heads * HEAD_DIM), BF16),
        scratch_shapes=[
            pltpu.VMEM((hps, nb // grp, HEAD_DIM, grp * blk), BF16),
            pltpu.VMEM((hps, nb, HEAD_DIM), F32),
            pltpu.VMEM((hps, nb // grp, nb // grp, 8, grp * blk), F32),
        ],
        compiler_params=_params("parallel", "parallel"),
        name="moba_attention",
    )(qk, qk, v)


def _pool_kernel(u_ref, halo_ref, w_ref, scale_ref, o_ref, ext_ref):
    ts = u_ref.shape[0]
    cg = w_ref.shape[1]
    s = pl.program_id(1)
    first = s == 0
    ext_ref[0:POOL_HALO, :] = jnp.where(first, 0.0, halo_ref[...])
    ext_ref[POOL_HALO:, :] = u_ref[...]
    pos1 = lax.broadcasted_iota(jnp.int32, (ts, cg), 0) + (s * ts + 1)
    for g, w in enumerate(POOL_WINDOWS):
        cs = slice(g * cg, (g + 1) * cg)
        ug = u_ref[:, cs]
        tot = ug
        for d in range(1, w):
            tot = tot + ext_ref[POOL_HALO - d:POOL_HALO - d + ts, cs]
        count = jnp.minimum(pos1, w).astype(F32)
        pooled = tot / count - ug
        out = jnp.dot(pooled.astype(BF16), w_ref[g], preferred_element_type=F32) * scale_ref[:, cs]
        o_ref[:, cs] = out.astype(o_ref.dtype)


def _pool_mixer(u, w_pool, pool_scale, *, batch, seq):
    P = u.shape[-1]
    G, cg, _ = w_pool.shape
    ts = _pick(seq, 512)
    hpt = ts // POOL_HALO
    return pl.pallas_call(
        _pool_kernel,
        grid=(batch, seq // ts),
        in_specs=[
            pl.BlockSpec((None, ts, P), lambda b, s: (b, s, 0)),
            pl.BlockSpec((None, POOL_HALO, P), lambda b, s: (b, jnp.maximum(s * hpt - 1, 0), 0)),
            pl.BlockSpec((G, cg, cg), lambda b, s: (0, 0, 0)),
            pl.BlockSpec((1, P), lambda b, s: (0, 0)),
        ],
        out_specs=pl.BlockSpec((None, ts, P), lambda b, s: (b, s, 0)),
        out_shape=jax.ShapeDtypeStruct((batch, seq, P), BF16),
        scratch_shapes=[pltpu.VMEM((POOL_HALO + ts, P), F32)],
        compiler_params=_params("parallel", "parallel"),
        name="pool_mixer",
    )(u, u, w_pool, pool_scale)


def _layer_norm_tiles(tile, nj, tn, g_ref, b_ref, o_ref):
    d = nj * tn
    tot = jnp.sum(tile(0), axis=1, keepdims=True)
    for jj in range(1, nj):
        tot = tot + jnp.sum(tile(jj), axis=1, keepdims=True)
    mu = tot * (1.0 / d)
    sq = jnp.sum(jnp.square(tile(0) - mu), axis=1, keepdims=True)
    for jj in range(1, nj):
        sq = sq + jnp.sum(jnp.square(tile(jj) - mu), axis=1, keepdims=True)
    rstd = lax.rsqrt(sq * (1.0 / d) + LN_EPS)
    for jj in range(nj):
        cs = slice(jj * tn, (jj + 1) * tn)
        o_ref[:, cs] = (tile(jj) - mu) * rstd * g_ref[:, cs] + b_ref[:, cs]


def _outproj_ln_kernel(a_ref, p_ref, w_ref, x_ref, g_ref, b_ref, ci_ref, o_ref, co_ref, pre_ref, *, alpha):
    j = pl.program_id(1)
    co_ref[...] = ci_ref[...].astype(co_ref.dtype)
    ka = a_ref.shape[1]
    mix = jnp.dot(a_ref[...], w_ref[0:ka, :], preferred_element_type=F32)
    mix = mix + jnp.dot(p_ref[...], w_ref[ka:, :], preferred_element_type=F32)
    pre_ref[j] = alpha * x_ref[...] + mix

    @pl.when(j == pl.num_programs(1) - 1)
    def _():
        nj, _, tn = pre_ref.shape
        _layer_norm_tiles(lambda jj: pre_ref[jj], nj, tn, g_ref, b_ref, o_ref)


def _outproj_ln(attn, pool, w_out, x2, g, b, w_cast, *, alpha):
    T, D = x2.shape
    ka, kp = attn.shape[1], pool.shape[1]
    tm = _pick(T, 512)
    tn = _pick(D, 512)
    grid = (T // tm, D // tn)
    cast_spec, cast_shape = _cast_stream(w_cast, grid)
    return pl.pallas_call(
        functools.partial(_outproj_ln_kernel, alpha=alpha),
        grid=grid,
        in_specs=[
            pl.BlockSpec((tm, ka), lambda i, j: (i, 0)),
            pl.BlockSpec((tm, kp), lambda i, j: (i, 0)),
            pl.BlockSpec((ka + kp, tn), lambda i, j: (0, j)),
            pl.BlockSpec((tm, tn), lambda i, j: (i, j)),
            pl.BlockSpec((1, D), lambda i, j: (0, 0)),
            pl.BlockSpec((1, D), lambda i, j: (0, 0)),
            cast_spec,
        ],
        out_specs=[pl.BlockSpec((tm, D), lambda i, j: (i, 0)), cast_spec],
        out_shape=[jax.ShapeDtypeStruct((T, D), F32), cast_shape],
        scratch_shapes=[pltpu.VMEM((D // tn, tm, tn), F32)],
        compiler_params=_params("parallel", "arbitrary"),
        name="outproj_ln1",
    )(attn, pool, w_out, x2, g, b, w_cast)


def _ffn_ln_kernel(h_ref, w1_ref, w2_ref, g_ref, b_ref, o_ref, hb_ref, *, alpha, tn):
    f = pl.program_id(1)
    nj = o_ref.shape[1] // tn

    @pl.when(f == 0)
    def _():
        hb_ref[...] = h_ref[...].astype(BF16)
        o_ref[...] = jnp.zeros_like(o_ref)

    hid = jnp.dot(hb_ref[...], w1_ref[...], preferred_element_type=F32)
    hid = jnp.square(jnp.maximum(hid, 0.0)).astype(BF16)
    for jj in range(nj):
        cs = slice(jj * tn, (jj + 1) * tn)
        o_ref[:, cs] += jnp.dot(hid, w2_ref[:, cs], preferred_element_type=F32)

    @pl.when(f == pl.num_programs(1) - 1)
    def _():
        o_ref[...] = alpha * h_ref[...] + o_ref[...]
        _layer_norm_tiles(lambda jj: o_ref[:, jj * tn:(jj + 1) * tn], nj, tn, g_ref, b_ref, o_ref)


def _ffn_ln(h, w1, w2, g, b, *, alpha):
    T, D = h.shape
    F = w1.shape[1]
    tm = _pick(T, 512)
    tf = _pick(F, 512)
    tn = _pick(D, 512)
    return pl.pallas_call(
        functools.partial(_ffn_ln_kernel, alpha=alpha, tn=tn),
        grid=(T // tm, F // tf),
        in_specs=[
            pl.BlockSpec((tm, D), lambda i, f: (i, 0), pipeline_mode=pl.Buffered(1)),
            pl.BlockSpec((D, tf), lambda i, f: (0, f)),
            pl.BlockSpec((tf, D), lambda i, f: (f, 0)),
            pl.BlockSpec((1, D), lambda i, f: (0, 0)),
            pl.BlockSpec((1, D), lambda i, f: (0, 0)),
        ],
        out_specs=pl.BlockSpec((tm, D), lambda i, f: (i, 0)),
        out_shape=jax.ShapeDtypeStruct((T, D), F32),
        scratch_shapes=[pltpu.VMEM((tm, D), BF16)],
        compiler_params=_params("parallel", "arbitrary"),
        name="ffn_ln2",
    )(h, w1, w2, g, b)


def _rope_tables(seq):
    inv_freq = 1.0 / (ROPE_THETA ** (jnp.arange(0, HEAD_DIM, 2, dtype=F32) / HEAD_DIM))
    ang = jnp.arange(seq, dtype=F32)[:, None] * inv_freq[None, :]
    cos, sin = jnp.cos(ang), jnp.sin(ang)
    return jnp.concatenate([cos, cos], axis=-1), jnp.concatenate([-sin, sin], axis=-1)


def kernel(x, w_in, w_pool, pool_scale, w_out, ln1_g, ln1_b, w_ff1, w_ff2, ln2_g, ln2_b):
    B, S, D = x.shape
    depth = w_in.shape[0]
    attn_w = D // 2
    n_heads = attn_w // HEAD_DIM
    pool_w = D - attn_w
    alpha = (2.0 * depth) ** 0.25
    cos, sin = _rope_tables(S)

    x2 = x.reshape(B * S, D)
    for l in range(depth):
        xb = x2.astype(BF16)
        w_in_b = w_in[l].astype(BF16)
        qk, w_ff1_b = _project_heads(xb, w_in_b, cos, sin, w_ff1[l], batch=B, seq=S, col0=0,
                                     ncols=2 * attn_w, rope=True)
        v, w_out_b = _project_heads(xb, w_in_b, cos, sin, w_out[l], batch=B, seq=S, col0=2 * attn_w,
                                    ncols=attn_w, rope=False)
        u = _project_plain(xb, w_in_b, col0=3 * attn_w, ncols=pool_w, out_dtype=F32)

        attn = _moba_attention(qk, v, seq=S, n_heads=n_heads)
        pool = _pool_mixer(u.reshape(B, S, pool_w), w_pool[l].astype(BF16),
                           pool_scale[l].reshape(1, pool_w), batch=B, seq=S)

        h, w_ff2_b = _outproj_ln(attn.reshape(B * S, attn_w), pool.reshape(B * S, pool_w), w_out_b, x2,
                                 ln1_g[l].reshape(1, D), ln1_b[l].reshape(1, D), w_ff2[l], alpha=alpha)
        x2 = _ffn_ln(h, w_ff1_b, w_ff2_b, ln2_g[l].reshape(1, D), ln2_b[l].reshape(1, D), alpha=alpha)
    return x2.reshape(B, S, D)
```

```python
import functools

import jax
import jax.numpy as jnp
from jax import lax
from jax.experimental import pallas as pl
from jax.experimental.pallas import tpu as pltpu

HEAD_DIM = 128
POOL_WINDOWS = (2, 4, 8, 16)
MOBA_BLOCK = 256
MOBA_TOPK = 3
MOBA_GROUP = 4
MOBA_HEADS_PER_STEP = 2
LOG2_E = 1.4426950408889634
ROPE_THETA = 10000.0
LN_EPS = 1e-5
POOL_HALO = 16

V7X_VMEM_LIMIT_BYTES = 60 * 1024 * 1024

F32 = jnp.float32
BF16 = jnp.bfloat16


def _pick(n, pref):
    t = min(pref, n)
    while n % t:
        t //= 2
    return t


def _params(*sem):
    return pltpu.CompilerParams(dimension_semantics=sem, vmem_limit_bytes=V7X_VMEM_LIMIT_BYTES)


def _proj_first_kernel(x_ref, w_ref, ci_ref, u_ref, xb_ref, co_ref):
    @pl.when(pl.program_id(1) == 0)
    def _():
        xb_ref[...] = x_ref[...].astype(xb_ref.dtype)
    co_ref[...] = ci_ref[...].astype(co_ref.dtype)
    u_ref[...] = jnp.dot(xb_ref[...], w_ref[...], preferred_element_type=F32)


def _proj_heads_kernel(*refs, rope):
    if rope:
        x_ref, w_ref, cos_ref, sin_ref, ci_ref, o_ref, co_ref = refs
    else:
        x_ref, w_ref, ci_ref, o_ref, co_ref = refs
    co_ref[...] = ci_ref[...].astype(co_ref.dtype)
    acc = jnp.dot(x_ref[...], w_ref[...], preferred_element_type=F32)
    for hh in range(o_ref.shape[0]):
        t = acc[:, hh * HEAD_DIM:(hh + 1) * HEAD_DIM]
        if rope:
            t = t * cos_ref[...] + pltpu.roll(t, HEAD_DIM // 2, 1) * sin_ref[...]
        o_ref[hh] = t.astype(o_ref.dtype)


def _cast_stream(w_f32, grid):
    R, C = w_f32.shape
    steps = grid[0] * grid[1]
    rb = R // steps
    assert R % steps == 0 and rb % 16 == 0, (R, steps)
    spec = pl.BlockSpec((rb, C), lambda i, j: (i * grid[1] + j, 0))
    return spec, jax.ShapeDtypeStruct((R, C), BF16)


def _project_first(x2, w_b, w_cast):
    T, D = x2.shape
    ncols = w_b.shape[1]
    tm = _pick(T, 512)
    tn = _pick(ncols, 1024)
    grid = (T // tm, ncols // tn)
    cast_spec, cast_shape = _cast_stream(w_cast, grid)
    return pl.pallas_call(
        _proj_first_kernel,
        grid=grid,
        in_specs=[
            pl.BlockSpec((tm, D), lambda i, j: (i, 0)),
            pl.BlockSpec((D, tn), lambda i, j: (0, j)),
            cast_spec,
        ],
        out_specs=[
            pl.BlockSpec((tm, tn), lambda i, j: (i, j)),
            pl.BlockSpec((tm, D), lambda i, j: (i, 0)),
            cast_spec,
        ],
        out_shape=[jax.ShapeDtypeStruct((T, ncols), F32), jax.ShapeDtypeStruct((T, D), BF16), cast_shape],
        compiler_params=_params("parallel", "arbitrary"),
        name="proj_first",
    )(x2, w_b, w_cast)


def _project_heads(xb, w, w_cast, *, batch, seq, col0, ncols, rope_tables=None):
    T, D = xb.shape
    tm = _pick(seq, 1024)
    tn = _pick(ncols // 2, 1024)
    spt = seq // tm
    hpt = tn // HEAD_DIM
    j0 = col0 // tn
    grid = (T // tm, ncols // tn)
    half = grid[1] // 2
    cast_spec, cast_shape = _cast_stream(w_cast, grid)
    table_spec = pl.BlockSpec((None, tm, HEAD_DIM), lambda i, j: (j // half, i % spt, 0))
    tables = () if rope_tables is None else tuple(rope_tables)
    return pl.pallas_call(
        functools.partial(_proj_heads_kernel, rope=rope_tables is not None),
        grid=grid,
        in_specs=[
            pl.BlockSpec((tm, D), lambda i, j: (i, 0)),
            pl.BlockSpec((D, tn), lambda i, j: (0, j + j0)),
            *([table_spec] * len(tables)),
            cast_spec,
        ],
        out_specs=[
            pl.BlockSpec((None, hpt, tm, HEAD_DIM), lambda i, j: (i // spt, j, i % spt, 0)),
            cast_spec,
        ],
        out_shape=[jax.ShapeDtypeStruct((batch, ncols // HEAD_DIM, seq, HEAD_DIM), BF16), cast_shape],
        compiler_params=_params("parallel", "arbitrary"),
        name="proj_heads" if rope_tables is None else "proj_rope_heads",
    )(xb, w, *tables, w_cast)


def _moba_kernel(q_ref, k_ref, v_ref, o_ref, vtg_ref, kmean_ref, bias_ref):
    blk, grp = MOBA_BLOCK, MOBA_GROUP
    quad = grp * blk
    hps, seq, _ = k_ref.shape
    nb = seq // blk
    ngrp = nb // grp
    nt = (((1,), (1,)), ((), ()))

    def setup(h):
        for j in range(nb):
            kj = k_ref[h, j * blk:(j + 1) * blk, :].astype(F32)
            kmean_ref[h, j:j + 1, :] = jnp.sum(kj, axis=0, keepdims=True) * (1.0 / blk)
            vt = v_ref[h, j * blk:(j + 1) * blk, :].astype(F32).T.astype(BF16)
            vtg_ref[h, j // grp, 0:HEAD_DIM, (j % grp) * blk:(j % grp + 1) * blk] = vt
        ones_row = lax.broadcasted_iota(jnp.int32, (vtg_ref.shape[2] - HEAD_DIM, quad), 0) == 0
        for g in range(ngrp):
            vtg_ref[h, g, HEAD_DIM:, :] = jnp.where(ones_row, 1.0, 0.0).astype(BF16)
        gate = lax.dot_general(kmean_ref[h].astype(BF16), q_ref[h], nt, preferred_element_type=F32)
        row = lax.broadcasted_iota(jnp.int32, gate.shape, 0)
        qblk = lax.shift_right_logical(lax.broadcasted_iota(jnp.int32, gate.shape, 1), blk.bit_length() - 1)
        past = row < qblk
        gate = jnp.where(past, gate, -jnp.inf)
        rank = jnp.zeros(gate.shape, jnp.int32)
        for jp in range(nb):
            gj = gate[jp:jp + 1, :]
            beats = (gj > gate) | ((gj == gate) & (jp < row))
            rank = rank + beats.astype(jnp.int32)
        bias = jnp.where(past & (rank < MOBA_TOPK), 0.0, -jnp.inf).astype(F32)
        for g in range(ngrp):
            for qd in range(g, ngrp):
                bias_ref[h, g, qd, 0:grp, :] = bias[g * grp:(g + 1) * grp, qd * quad:(qd + 1) * quad]

    for h in range(hps):
        setup(h)

    kpos = lax.broadcasted_iota(jnp.int32, (blk, blk), 0)
    qpos = lax.broadcasted_iota(jnp.int32, (blk, blk), 1)
    causal = kpos <= qpos

    def scores(k_rows, q):
        return lax.dot_general(k_rows, q, nt, preferred_element_type=F32)

    def tiles_max(tiles):
        m = jnp.max(tiles[0], axis=0, keepdims=True)
        for t in tiles[1:]:
            m = jnp.maximum(m, jnp.max(t, axis=0, keepdims=True))
        return m

    heads = range(hps)

    def diagonal_group(qd, q0):
        s = [[scores(k_ref[h, pl.ds(q0, (r + 1) * blk), :],
                     q_ref[h, pl.ds(pl.multiple_of(q0 + r * blk, blk), blk), :]) for r in range(grp)]
             for h in heads]
        m, p = [], []
        for h in heads:
            mr, pr = [], []
            for r in range(grp):
                brow = bias_ref[h, qd, qd, 0:grp, r * blk:(r + 1) * blk]
                tiles = [s[h][r][t * blk:(t + 1) * blk] + brow[t:t + 1, :] for t in range(r)]
                tiles.append(jnp.where(causal, s[h][r][r * blk:(r + 1) * blk], -jnp.inf))
                mr.append(tiles_max(tiles))
                pr.append(jnp.concatenate([jnp.exp2(t - mr[r]) for t in tiles], axis=0).astype(BF16))
            m.append(jnp.concatenate(mr, axis=1))
            p.append(pr)
        acc = [jnp.concatenate(
            [jnp.dot(vtg_ref[h, qd, :, 0:(r + 1) * blk], p[h][r], preferred_element_type=F32)
             for r in range(grp)], axis=1) for h in heads]
        return tuple((m[h], acc[h]) for h in heads)

    def past_group(qd, q0, g, state):
        k0 = pl.multiple_of(g * quad, quad)
        s = [scores(k_ref[h, pl.ds(k0, quad), :], q_ref[h, pl.ds(q0, quad), :]) for h in heads]
        stats, p = [], []
        for h in heads:
            m = state[h][0]
            brow = bias_ref[h, g, qd, 0:grp, :]
            tiles = [s[h][t * blk:(t + 1) * blk] + brow[t:t + 1, :] for t in range(grp)]
            m_new = jnp.maximum(m, tiles_max(tiles))
            p.append(jnp.concatenate([jnp.exp2(t - m_new) for t in tiles], axis=0).astype(BF16))
            stats.append((m_new, jnp.exp2(m - m_new)))
        pv = [jnp.dot(vtg_ref[h, g], p[h], preferred_element_type=F32) for h in heads]
        return tuple((stats[h][0], stats[h][1] * state[h][1] + pv[h]) for h in heads)

    def query_quad(qd, _):
        q0 = pl.multiple_of(qd * quad, quad)
        state = lax.fori_loop(0, qd, functools.partial(past_group, qd, q0), diagonal_group(qd, q0))
        for h, (_, acc) in enumerate(state):
            out = acc[0:HEAD_DIM] / acc[HEAD_DIM:HEAD_DIM + 1]
            o_ref[pl.ds(q0, quad), h * HEAD_DIM:(h + 1) * HEAD_DIM] = out.T.astype(o_ref.dtype)
        return 0

    lax.fori_loop(0, ngrp, query_quad, 0)


def _moba_attention(qk, v, *, seq, n_heads):
    batch = qk.shape[0]
    blk, grp = MOBA_BLOCK, MOBA_GROUP
    hps = _pick(n_heads, MOBA_HEADS_PER_STEP)
    nb = seq // blk
    assert seq % (blk * grp) == 0 and blk & (blk - 1) == 0 and grp & (grp - 1) == 0
    heads = lambda off: pl.BlockSpec((None, hps, seq, HEAD_DIM), lambda b, h: (b, h + off, 0, 0))
    return pl.pallas_call(
        _moba_kernel,
        grid=(batch, n_heads // hps),
        in_specs=[heads(0), heads(n_heads // hps), heads(0)],
        out_specs=pl.BlockSpec((None, seq, hps * HEAD_DIM), lambda b, h: (b, 0, h)),
        out_shape=jax.ShapeDtypeStruct((batch, seq, n_heads * HEAD_DIM), BF16),
        scratch_shapes=[
            pltpu.VMEM((hps, nb // grp, HEAD_DIM + 16, grp * blk), BF16),
            pltpu.VMEM((hps, nb, HEAD_DIM), F32),
            pltpu.VMEM((hps, nb // grp, nb // grp, 8, grp * blk), F32),
        ],
        compiler_params=_params("parallel", "parallel"),
        name="moba_attention",
    )(qk, qk, v)


def _pool_kernel(u_ref, halo_ref, w_ref, scale_ref, o_ref, ext_ref):
    ts = u_ref.shape[0]
    cg = w_ref.shape[1]
    s = pl.program_id(1)
    first = s == 0
    ext_ref[0:POOL_HALO, :] = jnp.where(first, 0.0, halo_ref[...])
    ext_ref[POOL_HALO:, :] = u_ref[...]
    pos1 = lax.broadcasted_iota(jnp.int32, (ts, cg), 0) + (s * ts + 1)
    for g, w in enumerate(POOL_WINDOWS):
        cs = slice(g * cg, (g + 1) * cg)
        ug = u_ref[:, cs]
        tot = ug
        for d in range(1, w):
            tot = tot + ext_ref[POOL_HALO - d:POOL_HALO - d + ts, cs]
        count = jnp.minimum(pos1, w).astype(F32)
        pooled = tot / count - ug
        out = jnp.dot(pooled.astype(BF16), w_ref[g], preferred_element_type=F32) * scale_ref[:, cs]
        o_ref[:, cs] = out.astype(o_ref.dtype)


def _pool_mixer(u, w_pool, pool_scale, *, batch, seq):
    P = u.shape[-1]
    G, cg, _ = w_pool.shape
    ts = _pick(seq, 512)
    hpt = ts // POOL_HALO
    return pl.pallas_call(
        _pool_kernel,
        grid=(batch, seq // ts),
        in_specs=[
            pl.BlockSpec((None, ts, P), lambda b, s: (b, s, 0)),
            pl.BlockSpec((None, POOL_HALO, P), lambda b, s: (b, jnp.maximum(s * hpt - 1, 0), 0)),
            pl.BlockSpec((G, cg, cg), lambda b, s: (0, 0, 0)),
            pl.BlockSpec((1, P), lambda b, s: (0, 0)),
        ],
        out_specs=pl.BlockSpec((None, ts, P), lambda b, s: (b, s, 0)),
        out_shape=jax.ShapeDtypeStruct((batch, seq, P), BF16),
        scratch_shapes=[pltpu.VMEM((POOL_HALO + ts, P), F32)],
        compiler_params=_params("parallel", "parallel"),
        name="pool_mixer",
    )(u, u, w_pool, pool_scale)


def _layer_norm_tiles(tile, nj, tn, g_ref, b_ref, o_ref):
    d = nj * tn
    tot = jnp.sum(tile(0), axis=1, keepdims=True)
    for jj in range(1, nj):
        tot = tot + jnp.sum(tile(jj), axis=1, keepdims=True)
    mu = tot * (1.0 / d)
    sq = jnp.sum(jnp.square(tile(0) - mu), axis=1, keepdims=True)
    for jj in range(1, nj):
        sq = sq + jnp.sum(jnp.square(tile(jj) - mu), axis=1, keepdims=True)
    rstd = lax.rsqrt(sq * (1.0 / d) + LN_EPS)
    for jj in range(nj):
        cs = slice(jj * tn, (jj + 1) * tn)
        o_ref[:, cs] = (tile(jj) - mu) * rstd * g_ref[:, cs] + b_ref[:, cs]


def _outproj_ln_kernel(a_ref, p_ref, w_ref, x_ref, g_ref, b_ref, ci_ref, o_ref, co_ref, pre_ref, *, alpha):
    j = pl.program_id(1)
    co_ref[...] = ci_ref[...].astype(co_ref.dtype)
    ka = a_ref.shape[1]
    mix = jnp.dot(a_ref[...], w_ref[0:ka, :], preferred_element_type=F32)
    mix = mix + jnp.dot(p_ref[...], w_ref[ka:, :], preferred_element_type=F32)
    pre_ref[j] = alpha * x_ref[...] + mix

    @pl.when(j == pl.num_programs(1) - 1)
    def _():
        nj, _, tn = pre_ref.shape
        _layer_norm_tiles(lambda jj: pre_ref[jj], nj, tn, g_ref, b_ref, o_ref)


def _outproj_ln(attn, pool, w_out, x2, g, b, w_cast, *, alpha):
    T, D = x2.shape
    ka, kp = attn.shape[1], pool.shape[1]
    tm = _pick(T, 512)
    tn = _pick(D, 512)
    grid = (T // tm, D // tn)
    cast_spec, cast_shape = _cast_stream(w_cast, grid)
    return pl.pallas_call(
        functools.partial(_outproj_ln_kernel, alpha=alpha),
        grid=grid,
        in_specs=[
            pl.BlockSpec((tm, ka), lambda i, j: (i, 0)),
            pl.BlockSpec((tm, kp), lambda i, j: (i, 0)),
            pl.BlockSpec((ka + kp, tn), lambda i, j: (0, j)),
            pl.BlockSpec((tm, tn), lambda i, j: (i, j)),
            pl.BlockSpec((1, D), lambda i, j: (0, 0)),
            pl.BlockSpec((1, D), lambda i, j: (0, 0)),
            cast_spec,
        ],
        out_specs=[pl.BlockSpec((tm, D), lambda i, j: (i, 0)), cast_spec],
        out_shape=[jax.ShapeDtypeStruct((T, D), F32), cast_shape],
        scratch_shapes=[pltpu.VMEM((D // tn, tm, tn), F32)],
        compiler_params=_params("parallel", "arbitrary"),
        name="outproj_ln1",
    )(attn, pool, w_out, x2, g, b, w_cast)


def _ffn_ln_kernel(h_ref, w1_ref, w2_ref, g_ref, b_ref, o_ref, hb_ref, *, alpha, tn):
    f = pl.program_id(1)
    nj = o_ref.shape[1] // tn

    @pl.when(f == 0)
    def _():
        hb_ref[...] = h_ref[...].astype(BF16)
        o_ref[...] = jnp.zeros_like(o_ref)

    hid = jnp.dot(hb_ref[...], w1_ref[...], preferred_element_type=F32)
    hid = jnp.square(jnp.maximum(hid, 0.0)).astype(BF16)
    for jj in range(nj):
        cs = slice(jj * tn, (jj + 1) * tn)
        o_ref[:, cs] += jnp.dot(hid, w2_ref[:, cs], preferred_element_type=F32)

    @pl.when(f == pl.num_programs(1) - 1)
    def _():
        o_ref[...] = alpha * h_ref[...] + o_ref[...]
        _layer_norm_tiles(lambda jj: o_ref[:, jj * tn:(jj + 1) * tn], nj, tn, g_ref, b_ref, o_ref)


def _ffn_ln(h, w1, w2, g, b, *, alpha):
    T, D = h.shape
    F = w1.shape[1]
    tm = _pick(T, 512)
    tf = _pick(F, 512)
    tn = _pick(D, 512)
    return pl.pallas_call(
        functools.partial(_ffn_ln_kernel, alpha=alpha, tn=tn),
        grid=(T // tm, F // tf),
        in_specs=[
            pl.BlockSpec((tm, D), lambda i, f: (i, 0), pipeline_mode=pl.Buffered(1)),
            pl.BlockSpec((D, tf), lambda i, f: (0, f)),
            pl.BlockSpec((tf, D), lambda i, f: (f, 0)),
            pl.BlockSpec((1, D), lambda i, f: (0, 0)),
            pl.BlockSpec((1, D), lambda i, f: (0, 0)),
        ],
        out_specs=pl.BlockSpec((tm, D), lambda i, f: (i, 0)),
        out_shape=jax.ShapeDtypeStruct((T, D), F32),
        scratch_shapes=[pltpu.VMEM((tm, D), BF16)],
        compiler_params=_params("parallel", "arbitrary"),
        name="ffn_ln2",
    )(h, w1, w2, g, b)


def _rope_tables(seq, q_scale):
    inv_freq = 1.0 / (ROPE_THETA ** (jnp.arange(0, HEAD_DIM, 2, dtype=F32) / HEAD_DIM))
    ang = jnp.arange(seq, dtype=F32)[:, None] * inv_freq[None, :]
    cos, sin = jnp.cos(ang), jnp.sin(ang)
    cos = jnp.concatenate([cos, cos], axis=-1)
    sin = jnp.concatenate([-sin, sin], axis=-1)
    return jnp.stack([q_scale * cos, cos]), jnp.stack([q_scale * sin, sin])


def kernel(x, w_in, w_pool, pool_scale, w_out, ln1_g, ln1_b, w_ff1, w_ff2, ln2_g, ln2_b):
    B, S, D = x.shape
    depth = w_in.shape[0]
    attn_w = D // 2
    n_heads = attn_w // HEAD_DIM
    pool_w = D - attn_w
    alpha = (2.0 * depth) ** 0.25
    rope_tables = _rope_tables(S, HEAD_DIM ** -0.5 * LOG2_E)

    x2 = x.reshape(B * S, D)
    for l in range(depth):
        w_u_b = w_in[l][:, 3 * attn_w:].astype(BF16)
        u, xb, w_in_b = _project_first(x2, w_u_b, w_in[l])
        qk, w_ff1_b = _project_heads(xb, w_in_b, w_ff1[l], batch=B, seq=S, col0=0, ncols=2 * attn_w,
                                     rope_tables=rope_tables)
        v, w_out_b = _project_heads(xb, w_in_b, w_out[l], batch=B, seq=S, col0=2 * attn_w, ncols=attn_w)

        attn = _moba_attention(qk, v, seq=S, n_heads=n_heads)
        pool = _pool_mixer(u.reshape(B, S, pool_w), w_pool[l].astype(BF16),
                           pool_scale[l].reshape(1, pool_w), batch=B, seq=S)

        h, w_ff2_b = _outproj_ln(attn.reshape(B * S, attn_w), pool.reshape(B * S, pool_w), w_out_b, x2,
                                 ln1_g[l].reshape(1, D), ln1_b[l].reshape(1, D), w_ff2[l], alpha=alpha)
        x2 = _ffn_ln(h, w_ff1_b, w_ff2_b, ln2_g[l].reshape(1, D), ln2_b[l].reshape(1, D), alpha=alpha)
    return x2.reshape(B, S, D)
```

```python
import functools

import jax
import jax.numpy as jnp
from jax import lax
from jax.experimental import pallas as pl
from jax.experimental.pallas import tpu as pltpu

HEAD_DIM = 128
POOL_WINDOWS = (2, 4, 8, 16)
MOBA_BLOCK = 256
MOBA_TOPK = 3
MOBA_GROUP = 4
MOBA_HEADS_PER_STEP = 2
LOG2_E = 1.4426950408889634
ROPE_THETA = 10000.0
LN_EPS = 1e-5
POOL_HALO = 16

V7X_VMEM_LIMIT_BYTES = 60 * 1024 * 1024

F32 = jnp.float32
BF16 = jnp.bfloat16


def _pick(n, pref):
    t = min(pref, n)
    while n % t:
        t //= 2
    return t


def _params(*sem):
    return pltpu.CompilerParams(dimension_semantics=sem, vmem_limit_bytes=V7X_VMEM_LIMIT_BYTES)


def _proj_first_kernel(x_ref, w_ref, u_ref, xb_ref):
    xb = x_ref[...].astype(xb_ref.dtype)
    xb_ref[...] = xb
    u_ref[...] = jnp.dot(xb, w_ref[...], preferred_element_type=F32)


def _proj_heads_kernel(*refs, rope):
    if rope:
        x_ref, w_ref, cos_ref, sin_ref, ci_ref, o_ref, co_ref = refs
    else:
        x_ref, w_ref, ci_ref, o_ref, co_ref = refs
    co_ref[...] = ci_ref[...].astype(co_ref.dtype)
    acc = jnp.dot(x_ref[...], w_ref[...], preferred_element_type=F32)
    for hh in range(o_ref.shape[0]):
        t = acc[:, hh * HEAD_DIM:(hh + 1) * HEAD_DIM]
        if rope:
            t = t * cos_ref[...] + pltpu.roll(t, HEAD_DIM // 2, 1) * sin_ref[...]
        o_ref[hh] = t.astype(o_ref.dtype)


def _cast_stream(w_f32, grid):
    R, C = w_f32.shape
    steps = grid[0] * grid[1]
    rb = R // steps
    assert R % steps == 0 and rb % 16 == 0, (R, steps)
    spec = pl.BlockSpec((rb, C), lambda i, j: (i * grid[1] + j, 0))
    return spec, jax.ShapeDtypeStruct((R, C), BF16)


def _project_first(x2, w, *, col0, ncols):
    T, D = x2.shape
    tm = _pick(T, 512)
    return pl.pallas_call(
        _proj_first_kernel,
        grid=(T // tm,),
        in_specs=[
            pl.BlockSpec((tm, D), lambda i: (i, 0)),
            pl.BlockSpec((D, ncols), lambda i: (0, col0 // ncols), pipeline_mode=pl.Buffered(1)),
        ],
        out_specs=[pl.BlockSpec((tm, ncols), lambda i: (i, 0)), pl.BlockSpec((tm, D), lambda i: (i, 0))],
        out_shape=[jax.ShapeDtypeStruct((T, ncols), F32), jax.ShapeDtypeStruct((T, D), BF16)],
        compiler_params=_params("parallel"),
        name="proj_first",
    )(x2, w)


def _project_heads(xb, w, w_cast, *, batch, seq, col0, ncols, rope_tables=None):
    T, D = xb.shape
    tm = _pick(seq, 1024)
    tn = _pick(ncols // 2, 1024)
    spt = seq // tm
    hpt = tn // HEAD_DIM
    j0 = col0 // tn
    grid = (T // tm, ncols // tn)
    half = grid[1] // 2
    cast_spec, cast_shape = _cast_stream(w_cast, grid)
    table_spec = pl.BlockSpec((None, tm, HEAD_DIM), lambda i, j: (j // half, i % spt, 0))
    tables = () if rope_tables is None else tuple(rope_tables)
    return pl.pallas_call(
        functools.partial(_proj_heads_kernel, rope=rope_tables is not None),
        grid=grid,
        in_specs=[
            pl.BlockSpec((tm, D), lambda i, j: (i, 0)),
            pl.BlockSpec((D, tn), lambda i, j: (0, j + j0)),
            *([table_spec] * len(tables)),
            cast_spec,
        ],
        out_specs=[
            pl.BlockSpec((None, hpt, tm, HEAD_DIM), lambda i, j: (i // spt, j, i % spt, 0)),
            cast_spec,
        ],
        out_shape=[jax.ShapeDtypeStruct((batch, ncols // HEAD_DIM, seq, HEAD_DIM), BF16), cast_shape],
        compiler_params=_params("parallel", "arbitrary"),
        name="proj_heads" if rope_tables is None else "proj_rope_heads",
    )(xb, w, *tables, w_cast)


def _moba_kernel(q_ref, k_ref, v_ref, ci_ref, o_ref, co_ref, vtg_ref, kmean_ref, bias_ref):
    blk, grp = MOBA_BLOCK, MOBA_GROUP
    quad = grp * blk
    hps, seq, _ = k_ref.shape
    nb = seq // blk
    ngrp = nb // grp
    nt = (((1,), (1,)), ((), ()))

    co_ref[...] = ci_ref[...].astype(co_ref.dtype)

    def setup(h):
        for j in range(nb):
            kj = k_ref[h, j * blk:(j + 1) * blk, :].astype(F32)
            kmean_ref[h, j:j + 1, :] = jnp.sum(kj, axis=0, keepdims=True) * (1.0 / blk)
            vt = v_ref[h, j * blk:(j + 1) * blk, :].astype(F32).T.astype(BF16)
            vtg_ref[h, j // grp, 0:HEAD_DIM, (j % grp) * blk:(j % grp + 1) * blk] = vt
        ones_row = lax.broadcasted_iota(jnp.int32, (vtg_ref.shape[2] - HEAD_DIM, quad), 0) == 0
        for g in range(ngrp):
            vtg_ref[h, g, HEAD_DIM:, :] = jnp.where(ones_row, 1.0, 0.0).astype(BF16)
        gate = lax.dot_general(kmean_ref[h].astype(BF16), q_ref[h], nt, preferred_element_type=F32)
        row = lax.broadcasted_iota(jnp.int32, gate.shape, 0)
        qblk = lax.shift_right_logical(lax.broadcasted_iota(jnp.int32, gate.shape, 1), blk.bit_length() - 1)
        past = row < qblk
        gate = jnp.where(past, gate, -jnp.inf)
        rank = jnp.zeros(gate.shape, jnp.int32)
        for jp in range(nb):
            gj = gate[jp:jp + 1, :]
            beats = (gj > gate) | ((gj == gate) & (jp < row))
            rank = rank + beats.astype(jnp.int32)
        bias = jnp.where(past & (rank < MOBA_TOPK), 0.0, -jnp.inf).astype(F32)
        for g in range(ngrp):
            for qd in range(g, ngrp):
                bias_ref[h, g, qd, 0:grp, :] = bias[g * grp:(g + 1) * grp, qd * quad:(qd + 1) * quad]

    for h in range(hps):
        setup(h)

    kpos = lax.broadcasted_iota(jnp.int32, (blk, blk), 0)
    qpos = lax.broadcasted_iota(jnp.int32, (blk, blk), 1)
    causal = kpos <= qpos

    def scores(k_rows, q):
        return lax.dot_general(k_rows, q, nt, preferred_element_type=F32)

    def tiles_max(tiles):
        m = jnp.max(tiles[0], axis=0, keepdims=True)
        for t in tiles[1:]:
            m = jnp.maximum(m, jnp.max(t, axis=0, keepdims=True))
        return m

    heads = range(hps)

    def diagonal_group(qd, q0):
        s = [[scores(k_ref[h, pl.ds(q0, (r + 1) * blk), :],
                     q_ref[h, pl.ds(pl.multiple_of(q0 + r * blk, blk), blk), :]) for r in range(grp)]
             for h in heads]
        m, p = [], []
        for h in heads:
            mr, pr = [], []
            for r in range(grp):
                brow = bias_ref[h, qd, qd, 0:grp, r * blk:(r + 1) * blk]
                tiles = [s[h][r][t * blk:(t + 1) * blk] + brow[t:t + 1, :] for t in range(r)]
                tiles.append(jnp.where(causal, s[h][r][r * blk:(r + 1) * blk], -jnp.inf))
                mr.append(tiles_max(tiles))
                pr.append(jnp.concatenate([jnp.exp2(t - mr[r]) for t in tiles], axis=0).astype(BF16))
            m.append(jnp.concatenate(mr, axis=1))
            p.append(pr)
        acc = [jnp.concatenate(
            [jnp.dot(vtg_ref[h, qd, :, 0:(r + 1) * blk], p[h][r], preferred_element_type=F32)
             for r in range(grp)], axis=1) for h in heads]
        return tuple((m[h], acc[h]) for h in heads)

    def past_group(qd, q0, g, state):
        k0 = pl.multiple_of(g * quad, quad)
        s = [scores(k_ref[h, pl.ds(k0, quad), :], q_ref[h, pl.ds(q0, quad), :]) for h in heads]
        stats, p = [], []
        for h in heads:
            m = state[h][0]
            brow = bias_ref[h, g, qd, 0:grp, :]
            tiles = [s[h][t * blk:(t + 1) * blk] + brow[t:t + 1, :] for t in range(grp)]
            m_new = jnp.maximum(m, tiles_max(tiles))
            p.append(jnp.concatenate([jnp.exp2(t - m_new) for t in tiles], axis=0).astype(BF16))
            stats.append((m_new, jnp.exp2(m - m_new)))
        pv = [jnp.dot(vtg_ref[h, g], p[h], preferred_element_type=F32) for h in heads]
        return tuple((stats[h][0], stats[h][1] * state[h][1] + pv[h]) for h in heads)

    def query_quad(qd, _):
        q0 = pl.multiple_of(qd * quad, quad)
        state = lax.fori_loop(0, qd, functools.partial(past_group, qd, q0), diagonal_group(qd, q0))
        for h, (_, acc) in enumerate(state):
            out = acc[0:HEAD_DIM] / acc[HEAD_DIM:HEAD_DIM + 1]
            o_ref[pl.ds(q0, quad), h * HEAD_DIM:(h + 1) * HEAD_DIM] = out.T.astype(o_ref.dtype)
        return 0

    lax.fori_loop(0, ngrp, query_quad, 0)


def _moba_attention(qk, v, w_cast, *, seq, n_heads):
    batch = qk.shape[0]
    blk, grp = MOBA_BLOCK, MOBA_GROUP
    hps = _pick(n_heads, MOBA_HEADS_PER_STEP)
    nb = seq // blk
    assert seq % (blk * grp) == 0 and blk & (blk - 1) == 0 and grp & (grp - 1) == 0
    heads = lambda off: pl.BlockSpec((None, hps, seq, HEAD_DIM), lambda b, h: (b, h + off, 0, 0))
    grid = (batch, n_heads // hps)
    cast_spec, cast_shape = _cast_stream(w_cast, grid)
    return pl.pallas_call(
        _moba_kernel,
        grid=grid,
        in_specs=[heads(0), heads(n_heads // hps), heads(0), cast_spec],
        out_specs=[pl.BlockSpec((None, seq, hps * HEAD_DIM), lambda b, h: (b, 0, h)), cast_spec],
        out_shape=[jax.ShapeDtypeStruct((batch, seq, n_heads * HEAD_DIM), BF16), cast_shape],
        scratch_shapes=[
            pltpu.VMEM((hps, nb // grp, HEAD_DIM + 16, grp * blk), BF16),
            pltpu.VMEM((hps, nb, HEAD_DIM), F32),
            pltpu.VMEM((hps, nb // grp, nb // grp, 8, grp * blk), F32),
        ],
        compiler_params=_params("parallel", "parallel"),
        name="moba_attention",
    )(qk, qk, v, w_cast)


def _pool_kernel(u_ref, halo_ref, w_ref, scale_ref, o_ref, ext_ref):
    ts = u_ref.shape[0]
    cg = w_ref.shape[1]
    s = pl.program_id(1)
    first = s == 0
    ext_ref[0:POOL_HALO, :] = jnp.where(first, 0.0, halo_ref[...])
    ext_ref[POOL_HALO:, :] = u_ref[...]
    pos1 = lax.broadcasted_iota(jnp.int32, (ts, cg), 0) + (s * ts + 1)
    for g, w in enumerate(POOL_WINDOWS):
        cs = slice(g * cg, (g + 1) * cg)
        ug = u_ref[:, cs]
        tot = ug
        for d in range(1, w):
            tot = tot + ext_ref[POOL_HALO - d:POOL_HALO - d + ts, cs]
        count = jnp.minimum(pos1, w).astype(F32)
        pooled = tot / count - ug
        out = jnp.dot(pooled.astype(BF16), w_ref[g], preferred_element_type=F32) * scale_ref[:, cs]
        o_ref[:, cs] = out.astype(o_ref.dtype)


def _pool_mixer(u, w_pool, pool_scale, *, batch, seq):
    P = u.shape[-1]
    G, cg, _ = w_pool.shape
    ts = _pick(seq, 512)
    hpt = ts // POOL_HALO
    return pl.pallas_call(
        _pool_kernel,
        grid=(batch, seq // ts),
        in_specs=[
            pl.BlockSpec((None, ts, P), lambda b, s: (b, s, 0)),
            pl.BlockSpec((None, POOL_HALO, P), lambda b, s: (b, jnp.maximum(s * hpt - 1, 0), 0)),
            pl.BlockSpec((G, cg, cg), lambda b, s: (0, 0, 0)),
            pl.BlockSpec((1, P), lambda b, s: (0, 0)),
        ],
        out_specs=pl.BlockSpec((None, ts, P), lambda b, s: (b, s, 0)),
        out_shape=jax.ShapeDtypeStruct((batch, seq, P), BF16),
        scratch_shapes=[pltpu.VMEM((POOL_HALO + ts, P), F32)],
        compiler_params=_params("parallel", "parallel"),
        name="pool_mixer",
    )(u, u, w_pool, pool_scale)


def _layer_norm_tiles(tile, nj, tn, g_ref, b_ref, o_ref):
    d = nj * tn
    tot = jnp.sum(tile(0), axis=1, keepdims=True)
    for jj in range(1, nj):
        tot = tot + jnp.sum(tile(jj), axis=1, keepdims=True)
    mu = tot * (1.0 / d)
    sq = jnp.sum(jnp.square(tile(0) - mu), axis=1, keepdims=True)
    for jj in range(1, nj):
        sq = sq + jnp.sum(jnp.square(tile(jj) - mu), axis=1, keepdims=True)
    rstd = lax.rsqrt(sq * (1.0 / d) + LN_EPS)
    for jj in range(nj):
        cs = slice(jj * tn, (jj + 1) * tn)
        o_ref[:, cs] = (tile(jj) - mu) * rstd * g_ref[:, cs] + b_ref[:, cs]


def _outproj_ln_kernel(a_ref, p_ref, w_ref, x_ref, g_ref, b_ref, o_ref, pre_ref, *, alpha):
    j = pl.program_id(1)
    ka = a_ref.shape[1]
    mix = jnp.dot(a_ref[...], w_ref[0:ka, :], preferred_element_type=F32)
    mix = mix + jnp.dot(p_ref[...], w_ref[ka:, :], preferred_element_type=F32)
    pre_ref[j] = alpha * x_ref[...] + mix

    @pl.when(j == pl.num_programs(1) - 1)
    def _():
        nj, _, tn = pre_ref.shape
        _layer_norm_tiles(lambda jj: pre_ref[jj], nj, tn, g_ref, b_ref, o_ref)


def _outproj_ln(attn, pool, w_out, x2, g, b, *, alpha):
    T, D = x2.shape
    ka, kp = attn.shape[1], pool.shape[1]
    tm = _pick(T, 512)
    tn = _pick(D, 512)
    return pl.pallas_call(
        functools.partial(_outproj_ln_kernel, alpha=alpha),
        grid=(T // tm, D // tn),
        in_specs=[
            pl.BlockSpec((tm, ka), lambda i, j: (i, 0)),
            pl.BlockSpec((tm, kp), lambda i, j: (i, 0)),
            pl.BlockSpec((ka + kp, tn), lambda i, j: (0, j)),
            pl.BlockSpec((tm, tn), lambda i, j: (i, j)),
            pl.BlockSpec((1, D), lambda i, j: (0, 0)),
            pl.BlockSpec((1, D), lambda i, j: (0, 0)),
        ],
        out_specs=pl.BlockSpec((tm, D), lambda i, j: (i, 0)),
        out_shape=jax.ShapeDtypeStruct((T, D), F32),
        scratch_shapes=[pltpu.VMEM((D // tn, tm, tn), F32)],
        compiler_params=_params("parallel", "arbitrary"),
        name="outproj_ln1",
    )(attn, pool, w_out, x2, g, b)


def _ffn_ln_kernel(h_ref, w1_ref, w2_ref, g_ref, b_ref, o_ref, hb_ref, *, alpha, tn):
    f = pl.program_id(1)
    nj = o_ref.shape[1] // tn

    @pl.when(f == 0)
    def _():
        hb_ref[...] = h_ref[...].astype(BF16)
        o_ref[...] = jnp.zeros_like(o_ref)

    hid = jnp.dot(hb_ref[...], w1_ref[...], preferred_element_type=F32)
    hid = jnp.square(jnp.maximum(hid, 0.0)).astype(BF16)
    for jj in range(nj):
        cs = slice(jj * tn, (jj + 1) * tn)
        o_ref[:, cs] += jnp.dot(hid, w2_ref[:, cs], preferred_element_type=F32)

    @pl.when(f == pl.num_programs(1) - 1)
    def _():
        o_ref[...] = alpha * h_ref[...] + o_ref[...]
        _layer_norm_tiles(lambda jj: o_ref[:, jj * tn:(jj + 1) * tn], nj, tn, g_ref, b_ref, o_ref)


def _ffn_ln(h, w1, w2, g, b, *, alpha):
    T, D = h.shape
    F = w1.shape[1]
    tm = _pick(T, 512)
    tf = _pick(F, 512)
    tn = _pick(D, 512)
    return pl.pallas_call(
        functools.partial(_ffn_ln_kernel, alpha=alpha, tn=tn),
        grid=(T // tm, F // tf),
        in_specs=[
            pl.BlockSpec((tm, D), lambda i, f: (i, 0), pipeline_mode=pl.Buffered(1)),
            pl.BlockSpec((D, tf), lambda i, f: (0, f)),
            pl.BlockSpec((tf, D), lambda i, f: (f, 0)),
            pl.BlockSpec((1, D), lambda i, f: (0, 0)),
            pl.BlockSpec((1, D), lambda i, f: (0, 0)),
        ],
        out_specs=pl.BlockSpec((tm, D), lambda i, f: (i, 0)),
        out_shape=jax.ShapeDtypeStruct((T, D), F32),
        scratch_shapes=[pltpu.VMEM((tm, D), BF16)],
        compiler_params=_params("parallel", "arbitrary"),
        name="ffn_ln2",
    )(h, w1, w2, g, b)


def _rope_tables(seq, q_scale):
    inv_freq = 1.0 / (ROPE_THETA ** (jnp.arange(0, HEAD_DIM, 2, dtype=F32) / HEAD_DIM))
    ang = jnp.arange(seq, dtype=F32)[:, None] * inv_freq[None, :]
    cos, sin = jnp.cos(ang), jnp.sin(ang)
    cos = jnp.concatenate([cos, cos], axis=-1)
    sin = jnp.concatenate([-sin, sin], axis=-1)
    return jnp.stack([q_scale * cos, cos]), jnp.stack([q_scale * sin, sin])


def kernel(x, w_in, w_pool, pool_scale, w_out, ln1_g, ln1_b, w_ff1, w_ff2, ln2_g, ln2_b):
    B, S, D = x.shape
    depth = w_in.shape[0]
    attn_w = D // 2
    n_heads = attn_w // HEAD_DIM
    pool_w = D - attn_w
    alpha = (2.0 * depth) ** 0.25
    rope_tables = _rope_tables(S, HEAD_DIM ** -0.5 * LOG2_E)

    x2 = x.reshape(B * S, D)
    for l in range(depth):
        w_in_b = w_in[l].astype(BF16)
        assert (3 * attn_w) % pool_w == 0
        u, xb = _project_first(x2, w_in_b, col0=3 * attn_w, ncols=pool_w)
        qk, w_ff1_b = _project_heads(xb, w_in_b, w_ff1[l], batch=B, seq=S, col0=0, ncols=2 * attn_w,
                                     rope_tables=rope_tables)
        v, w_out_b = _project_heads(xb, w_in_b, w_out[l], batch=B, seq=S, col0=2 * attn_w, ncols=attn_w)

        attn, w_ff2_b = _moba_attention(qk, v, w_ff2[l], seq=S, n_heads=n_heads)
        pool = _pool_mixer(u.reshape(B, S, pool_w), w_pool[l].astype(BF16),
                           pool_scale[l].reshape(1, pool_w), batch=B, seq=S)

        h = _outproj_ln(attn.reshape(B * S, attn_w), pool.reshape(B * S, pool_w), w_out_b, x2,
                        ln1_g[l].reshape(1, D), ln1_b[l].reshape(1, D), alpha=alpha)
        x2 = _ffn_ln(h, w_ff1_b, w_ff2_b, ln2_g[l].reshape(1, D), ln2_b[l].reshape(1, D), alpha=alpha)
    return x2.reshape(B, S, D)
```

```python
import functools

import jax
import jax.numpy as jnp
from jax import lax
from jax.experimental import pallas as pl
from jax.experimental.pallas import tpu as pltpu

HEAD_DIM = 128
POOL_WINDOWS = (2, 4, 8, 16)
MOBA_BLOCK = 256
MOBA_TOPK = 3
MOBA_GROUP = 4
MOBA_HEADS_PER_STEP = 2
LOG2_E = 1.4426950408889634
FFN_HIDDEN_TILE = 512
OUTPROJ_COL_TILE = 512
ROPE_THETA = 10000.0
LN_EPS = 1e-5
POOL_HALO = 16

V7X_VMEM_LIMIT_BYTES = 60 * 1024 * 1024

F32 = jnp.float32
BF16 = jnp.bfloat16


def _pick(n, pref):
    t = min(pref, n)
    while n % t:
        t //= 2
    return t


def _params(*sem):
    return pltpu.CompilerParams(dimension_semantics=sem, vmem_limit_bytes=V7X_VMEM_LIMIT_BYTES)


def _proj_first_kernel(x_ref, w_ref, u_ref, xb_ref):
    xb = x_ref[...].astype(xb_ref.dtype)
    xb_ref[...] = xb
    u_ref[...] = jnp.dot(xb, w_ref[...], preferred_element_type=F32)


def _proj_heads_kernel(*refs, rope):
    if rope:
        x_ref, w_ref, cos_ref, sin_ref, ci_ref, o_ref, co_ref = refs
    else:
        x_ref, w_ref, ci_ref, o_ref, co_ref = refs
    _cast_slab(ci_ref, co_ref)
    acc = jnp.dot(x_ref[...], w_ref[...], preferred_element_type=F32)
    for hh in range(o_ref.shape[0]):
        t = acc[:, hh * HEAD_DIM:(hh + 1) * HEAD_DIM]
        if rope:
            t = t * cos_ref[...] + pltpu.roll(t, HEAD_DIM // 2, 1) * sin_ref[...]
        o_ref[hh] = t.astype(o_ref.dtype)


def _cast_stream(w_f32, grid, col_tile=None):
    R, C = w_f32.shape
    steps = grid[0] * grid[1]
    rb = R // steps
    assert R % steps == 0 and rb % 16 == 0, (R, steps)
    step = lambda i, j: i * grid[1] + j
    in_spec = pl.BlockSpec((rb, C), lambda i, j: (step(i, j), 0))
    if col_tile is None:
        return in_spec, in_spec, jax.ShapeDtypeStruct((R, C), BF16)
    assert C % col_tile == 0
    out_spec = pl.BlockSpec((C // col_tile, rb, col_tile), lambda i, j: (0, step(i, j), 0))
    return in_spec, out_spec, jax.ShapeDtypeStruct((C // col_tile, R, col_tile), BF16)


def _cast_slab(ci_ref, co_ref):
    if len(co_ref.shape) == 2:
        co_ref[...] = ci_ref[...].astype(co_ref.dtype)
    else:
        ct = co_ref.shape[2]
        for c in range(co_ref.shape[0]):
            co_ref[c] = ci_ref[:, c * ct:(c + 1) * ct].astype(co_ref.dtype)


def _project_first(x2, w, *, col0, ncols):
    T, D = x2.shape
    tm = _pick(T, 512)
    return pl.pallas_call(
        _proj_first_kernel,
        grid=(T // tm,),
        in_specs=[
            pl.BlockSpec((tm, D), lambda i: (i, 0)),
            pl.BlockSpec((D, ncols), lambda i: (0, col0 // ncols), pipeline_mode=pl.Buffered(1)),
        ],
        out_specs=[pl.BlockSpec((tm, ncols), lambda i: (i, 0)), pl.BlockSpec((tm, D), lambda i: (i, 0))],
        out_shape=[jax.ShapeDtypeStruct((T, ncols), F32), jax.ShapeDtypeStruct((T, D), BF16)],
        compiler_params=_params("parallel"),
        name="proj_first",
    )(x2, w)


def _project_heads(xb, w, w_cast, *, batch, seq, col0, ncols, cast_col_tile, rope_tables=None):
    T, D = xb.shape
    tm = _pick(seq, 1024)
    tn = _pick(ncols // 2, 1024)
    spt = seq // tm
    hpt = tn // HEAD_DIM
    j0 = col0 // tn
    grid = (T // tm, ncols // tn)
    half = grid[1] // 2
    cast_in, cast_out, cast_shape = _cast_stream(w_cast, grid, cast_col_tile)
    table_spec = pl.BlockSpec((None, tm, HEAD_DIM), lambda i, j: (j // half, i % spt, 0))
    tables = () if rope_tables is None else tuple(rope_tables)
    return pl.pallas_call(
        functools.partial(_proj_heads_kernel, rope=rope_tables is not None),
        grid=grid,
        in_specs=[
            pl.BlockSpec((tm, D), lambda i, j: (i, 0)),
            pl.BlockSpec((D, tn), lambda i, j: (0, j + j0)),
            *([table_spec] * len(tables)),
            cast_in,
        ],
        out_specs=[
            pl.BlockSpec((None, hpt, tm, HEAD_DIM), lambda i, j: (i // spt, j, i % spt, 0)),
            cast_out,
        ],
        out_shape=[jax.ShapeDtypeStruct((batch, ncols // HEAD_DIM, seq, HEAD_DIM), BF16), cast_shape],
        compiler_params=_params("parallel", "arbitrary"),
        name="proj_heads" if rope_tables is None else "proj_rope_heads",
    )(xb, w, *tables, w_cast)


def _moba_kernel(q_ref, k_ref, v_ref, ci_ref, o_ref, co_ref, vtg_ref, kmean_ref, bias_ref):
    blk, grp = MOBA_BLOCK, MOBA_GROUP
    quad = grp * blk
    hps, seq, _ = k_ref.shape
    nb = seq // blk
    ngrp = nb // grp
    nt = (((1,), (1,)), ((), ()))

    _cast_slab(ci_ref, co_ref)

    def setup(h):
        for j in range(nb):
            kj = k_ref[h, j * blk:(j + 1) * blk, :].astype(F32)
            kmean_ref[h, j:j + 1, :] = jnp.sum(kj, axis=0, keepdims=True) * (1.0 / blk)
            vt = v_ref[h, j * blk:(j + 1) * blk, :].astype(F32).T.astype(BF16)
            vtg_ref[h, j // grp, 0:HEAD_DIM, (j % grp) * blk:(j % grp + 1) * blk] = vt
        ones_row = lax.broadcasted_iota(jnp.int32, (vtg_ref.shape[2] - HEAD_DIM, quad), 0) == 0
        for g in range(ngrp):
            vtg_ref[h, g, HEAD_DIM:, :] = jnp.where(ones_row, 1.0, 0.0).astype(BF16)
        gate = lax.dot_general(kmean_ref[h].astype(BF16), q_ref[h], nt, preferred_element_type=F32)
        row = lax.broadcasted_iota(jnp.int32, gate.shape, 0)
        qblk = lax.shift_right_logical(lax.broadcasted_iota(jnp.int32, gate.shape, 1), blk.bit_length() - 1)
        past = row < qblk
        gate = jnp.where(past, gate, -jnp.inf)
        rank = jnp.zeros(gate.shape, jnp.int32)
        for jp in range(nb):
            gj = gate[jp:jp + 1, :]
            beats = (gj > gate) | ((gj == gate) & (jp < row))
            rank = rank + beats.astype(jnp.int32)
        bias = jnp.where(past & (rank < MOBA_TOPK), 0.0, -jnp.inf).astype(F32)
        for g in range(ngrp):
            for qd in range(g, ngrp):
                bias_ref[h, g, qd, 0:grp, :] = bias[g * grp:(g + 1) * grp, qd * quad:(qd + 1) * quad]

    for h in range(hps):
        setup(h)

    kpos = lax.broadcasted_iota(jnp.int32, (blk, blk), 0)
    qpos = lax.broadcasted_iota(jnp.int32, (blk, blk), 1)
    causal = kpos <= qpos

    def scores(k_rows, q):
        return lax.dot_general(k_rows, q, nt, preferred_element_type=F32)

    def tiles_max(tiles):
        m = jnp.max(tiles[0], axis=0, keepdims=True)
        for t in tiles[1:]:
            m = jnp.maximum(m, jnp.max(t, axis=0, keepdims=True))
        return m

    heads = range(hps)

    def diagonal_group(qd, q0):
        s = [[scores(k_ref[h, pl.ds(q0, (r + 1) * blk), :],
                     q_ref[h, pl.ds(pl.multiple_of(q0 + r * blk, blk), blk), :]) for r in range(grp)]
             for h in heads]
        m, p = [], []
        for h in heads:
            mr, pr = [], []
            for r in range(grp):
                brow = bias_ref[h, qd, qd, 0:grp, r * blk:(r + 1) * blk]
                tiles = [s[h][r][t * blk:(t + 1) * blk] + brow[t:t + 1, :] for t in range(r)]
                tiles.append(jnp.where(causal, s[h][r][r * blk:(r + 1) * blk], -jnp.inf))
                mr.append(tiles_max(tiles))
                pr.append(jnp.concatenate([jnp.exp2(t - mr[r]) for t in tiles], axis=0).astype(BF16))
            m.append(jnp.concatenate(mr, axis=1))
            p.append(pr)
        acc = [jnp.concatenate(
            [jnp.dot(vtg_ref[h, qd, :, 0:(r + 1) * blk], p[h][r], preferred_element_type=F32)
             for r in range(grp)], axis=1) for h in heads]
        return tuple((m[h], acc[h]) for h in heads)

    def past_group(qd, q0, g, state):
        k0 = pl.multiple_of(g * quad, quad)
        s = [scores(k_ref[h, pl.ds(k0, quad), :], q_ref[h, pl.ds(q0, quad), :]) for h in heads]
        stats, p = [], []
        for h in heads:
            m = state[h][0]
            brow = bias_ref[h, g, qd, 0:grp, :]
            tiles = [s[h][t * blk:(t + 1) * blk] + brow[t:t + 1, :] for t in range(grp)]
            m_new = jnp.maximum(m, tiles_max(tiles))
            p.append(jnp.concatenate([jnp.exp2(t - m_new) for t in tiles], axis=0).astype(BF16))
            stats.append((m_new, jnp.exp2(m - m_new)))
        pv = [jnp.dot(vtg_ref[h, g], p[h], preferred_element_type=F32) for h in heads]
        return tuple((stats[h][0], stats[h][1] * state[h][1] + pv[h]) for h in heads)

    def query_quad(qd, _):
        q0 = pl.multiple_of(qd * quad, quad)
        state = lax.fori_loop(0, qd, functools.partial(past_group, qd, q0), diagonal_group(qd, q0))
        for h, (_, acc) in enumerate(state):
            out = acc[0:HEAD_DIM] / acc[HEAD_DIM:HEAD_DIM + 1]
            o_ref[pl.ds(q0, quad), h * HEAD_DIM:(h + 1) * HEAD_DIM] = out.T.astype(o_ref.dtype)
        return 0

    lax.fori_loop(0, ngrp, query_quad, 0)


def _moba_attention(qk, v, w_cast, *, seq, n_heads):
    batch = qk.shape[0]
    blk, grp = MOBA_BLOCK, MOBA_GROUP
    hps = _pick(n_heads, MOBA_HEADS_PER_STEP)
    nb = seq // blk
    assert seq % (blk * grp) == 0 and blk & (blk - 1) == 0 and grp & (grp - 1) == 0
    heads = lambda off: pl.BlockSpec((None, hps, seq, HEAD_DIM), lambda b, h: (b, h + off, 0, 0))
    grid = (batch, n_heads // hps)
    cast_in, cast_out, cast_shape = _cast_stream(w_cast, grid)
    return pl.pallas_call(
        _moba_kernel,
        grid=grid,
        in_specs=[heads(0), heads(n_heads // hps), heads(0), cast_in],
        out_specs=[pl.BlockSpec((None, seq, hps * HEAD_DIM), lambda b, h: (b, 0, h)), cast_out],
        out_shape=[jax.ShapeDtypeStruct((batch, seq, n_heads * HEAD_DIM), BF16), cast_shape],
        scratch_shapes=[
            pltpu.VMEM((hps, nb // grp, HEAD_DIM + 16, grp * blk), BF16),
            pltpu.VMEM((hps, nb, HEAD_DIM), F32),
            pltpu.VMEM((hps, nb // grp, nb // grp, 8, grp * blk), F32),
        ],
        compiler_params=_params("parallel", "parallel"),
        name="moba_attention",
    )(qk, qk, v, w_cast)


def _pool_kernel(u_ref, halo_ref, w_ref, scale_ref, o_ref, ext_ref):
    ts = u_ref.shape[0]
    cg = w_ref.shape[1]
    s = pl.program_id(1)
    first = s == 0
    ext_ref[0:POOL_HALO, :] = jnp.where(first, 0.0, halo_ref[...])
    ext_ref[POOL_HALO:, :] = u_ref[...]
    pos1 = lax.broadcasted_iota(jnp.int32, (ts, cg), 0) + (s * ts + 1)
    for g, w in enumerate(POOL_WINDOWS):
        cs = slice(g * cg, (g + 1) * cg)
        ug = u_ref[:, cs]
        tot = ug
        for d in range(1, w):
            tot = tot + ext_ref[POOL_HALO - d:POOL_HALO - d + ts, cs]
        count = jnp.minimum(pos1, w).astype(F32)
        pooled = tot / count - ug
        out = jnp.dot(pooled.astype(BF16), w_ref[g], preferred_element_type=F32) * scale_ref[:, cs]
        o_ref[:, cs] = out.astype(o_ref.dtype)


def _pool_mixer(u, w_pool, pool_scale, *, batch, seq):
    P = u.shape[-1]
    G, cg, _ = w_pool.shape
    ts = _pick(seq, 512)
    hpt = ts // POOL_HALO
    return pl.pallas_call(
        _pool_kernel,
        grid=(batch, seq // ts),
        in_specs=[
            pl.BlockSpec((None, ts, P), lambda b, s: (b, s, 0)),
            pl.BlockSpec((None, POOL_HALO, P), lambda b, s: (b, jnp.maximum(s * hpt - 1, 0), 0)),
            pl.BlockSpec((G, cg, cg), lambda b, s: (0, 0, 0)),
            pl.BlockSpec((1, P), lambda b, s: (0, 0)),
        ],
        out_specs=pl.BlockSpec((None, ts, P), lambda b, s: (b, s, 0)),
        out_shape=jax.ShapeDtypeStruct((batch, seq, P), BF16),
        scratch_shapes=[pltpu.VMEM((POOL_HALO + ts, P), F32)],
        compiler_params=_params("parallel", "parallel"),
        name="pool_mixer",
    )(u, u, w_pool, pool_scale)


def _layer_norm_tiles(tile, nj, tn, g_ref, b_ref, o_ref):
    d = nj * tn
    tot = jnp.sum(tile(0), axis=1, keepdims=True)
    for jj in range(1, nj):
        tot = tot + jnp.sum(tile(jj), axis=1, keepdims=True)
    mu = tot * (1.0 / d)
    sq = jnp.sum(jnp.square(tile(0) - mu), axis=1, keepdims=True)
    for jj in range(1, nj):
        sq = sq + jnp.sum(jnp.square(tile(jj) - mu), axis=1, keepdims=True)
    rstd = lax.rsqrt(sq * (1.0 / d) + LN_EPS)
    for jj in range(nj):
        cs = slice(jj * tn, (jj + 1) * tn)
        o_ref[:, cs] = (tile(jj) - mu) * rstd * g_ref[:, cs] + b_ref[:, cs]


def _outproj_ln_kernel(a_ref, p_ref, w_ref, x_ref, g_ref, b_ref, o_ref, pre_ref, *, alpha):
    j = pl.program_id(1)
    ka = a_ref.shape[1]
    mix = jnp.dot(a_ref[...], w_ref[0:ka, :], preferred_element_type=F32)
    mix = mix + jnp.dot(p_ref[...], w_ref[ka:, :], preferred_element_type=F32)
    pre_ref[j] = alpha * x_ref[...] + mix

    @pl.when(j == pl.num_programs(1) - 1)
    def _():
        nj, _, tn = pre_ref.shape
        _layer_norm_tiles(lambda jj: pre_ref[jj], nj, tn, g_ref, b_ref, o_ref)


def _outproj_ln(attn, pool, w_out_t, x2, g, b, *, alpha):
    T, D = x2.shape
    ka, kp = attn.shape[1], pool.shape[1]
    tm = _pick(T, 512)
    tn = w_out_t.shape[2]
    return pl.pallas_call(
        functools.partial(_outproj_ln_kernel, alpha=alpha),
        grid=(T // tm, D // tn),
        in_specs=[
            pl.BlockSpec((tm, ka), lambda i, j: (i, 0)),
            pl.BlockSpec((tm, kp), lambda i, j: (i, 0)),
            pl.BlockSpec((None, ka + kp, tn), lambda i, j: (j, 0, 0)),
            pl.BlockSpec((tm, tn), lambda i, j: (i, j)),
            pl.BlockSpec((1, D), lambda i, j: (0, 0)),
            pl.BlockSpec((1, D), lambda i, j: (0, 0)),
        ],
        out_specs=pl.BlockSpec((tm, D), lambda i, j: (i, 0)),
        out_shape=jax.ShapeDtypeStruct((T, D), F32),
        scratch_shapes=[pltpu.VMEM((D // tn, tm, tn), F32)],
        compiler_params=_params("parallel", "arbitrary"),
        name="outproj_ln1",
    )(attn, pool, w_out_t, x2, g, b)


def _ffn_ln_kernel(h_ref, w1_ref, w2_ref, g_ref, b_ref, o_ref, hb_ref, *, alpha, tn):
    f = pl.program_id(1)
    nj = o_ref.shape[1] // tn

    @pl.when(f == 0)
    def _():
        hb_ref[...] = h_ref[...].astype(BF16)
        o_ref[...] = jnp.zeros_like(o_ref)

    hid = jnp.dot(hb_ref[...], w1_ref[...], preferred_element_type=F32)
    hid = jnp.square(jnp.maximum(hid, 0.0)).astype(BF16)
    for jj in range(nj):
        cs = slice(jj * tn, (jj + 1) * tn)
        o_ref[:, cs] += jnp.dot(hid, w2_ref[:, cs], preferred_element_type=F32)

    @pl.when(f == pl.num_programs(1) - 1)
    def _():
        o_ref[...] = alpha * h_ref[...] + o_ref[...]
        _layer_norm_tiles(lambda jj: o_ref[:, jj * tn:(jj + 1) * tn], nj, tn, g_ref, b_ref, o_ref)


def _ffn_ln(h, w1_t, w2, g, b, *, alpha):
    T, D = h.shape
    nf, _, tf = w1_t.shape
    tm = _pick(T, 512)
    tn = _pick(D, 512)
    return pl.pallas_call(
        functools.partial(_ffn_ln_kernel, alpha=alpha, tn=tn),
        grid=(T // tm, nf),
        in_specs=[
            pl.BlockSpec((tm, D), lambda i, f: (i, 0), pipeline_mode=pl.Buffered(1)),
            pl.BlockSpec((None, D, tf), lambda i, f: (f, 0, 0)),
            pl.BlockSpec((tf, D), lambda i, f: (f, 0)),
            pl.BlockSpec((1, D), lambda i, f: (0, 0)),
            pl.BlockSpec((1, D), lambda i, f: (0, 0)),
        ],
        out_specs=pl.BlockSpec((tm, D), lambda i, f: (i, 0)),
        out_shape=jax.ShapeDtypeStruct((T, D), F32),
        scratch_shapes=[pltpu.VMEM((tm, D), BF16)],
        compiler_params=_params("parallel", "arbitrary"),
        name="ffn_ln2",
    )(h, w1_t, w2, g, b)


def _rope_tables(seq, q_scale):
    inv_freq = 1.0 / (ROPE_THETA ** (jnp.arange(0, HEAD_DIM, 2, dtype=F32) / HEAD_DIM))
    ang = jnp.arange(seq, dtype=F32)[:, None] * inv_freq[None, :]
    cos, sin = jnp.cos(ang), jnp.sin(ang)
    cos = jnp.concatenate([cos, cos], axis=-1)
    sin = jnp.concatenate([-sin, sin], axis=-1)
    return jnp.stack([q_scale * cos, cos]), jnp.stack([q_scale * sin, sin])


def kernel(x, w_in, w_pool, pool_scale, w_out, ln1_g, ln1_b, w_ff1, w_ff2, ln2_g, ln2_b):
    B, S, D = x.shape
    depth = w_in.shape[0]
    attn_w = D // 2
    n_heads = attn_w // HEAD_DIM
    pool_w = D - attn_w
    alpha = (2.0 * depth) ** 0.25
    rope_tables = _rope_tables(S, HEAD_DIM ** -0.5 * LOG2_E)

    x2 = x.reshape(B * S, D)
    for l in range(depth):
        w_in_b = w_in[l].astype(BF16)
        assert (3 * attn_w) % pool_w == 0
        u, xb = _project_first(x2, w_in_b, col0=3 * attn_w, ncols=pool_w)
        qk, w_ff1_t = _project_heads(xb, w_in_b, w_ff1[l], batch=B, seq=S, col0=0, ncols=2 * attn_w,
                                     cast_col_tile=_pick(w_ff1.shape[2], FFN_HIDDEN_TILE),
                                     rope_tables=rope_tables)
        v, w_out_t = _project_heads(xb, w_in_b, w_out[l], batch=B, seq=S, col0=2 * attn_w, ncols=attn_w,
                                    cast_col_tile=_pick(D, OUTPROJ_COL_TILE))

        attn, w_ff2_b = _moba_attention(qk, v, w_ff2[l], seq=S, n_heads=n_heads)
        pool = _pool_mixer(u.reshape(B, S, pool_w), w_pool[l].astype(BF16),
                           pool_scale[l].reshape(1, pool_w), batch=B, seq=S)

        h = _outproj_ln(attn.reshape(B * S, attn_w), pool.reshape(B * S, pool_w), w_out_t, x2,
                        ln1_g[l].reshape(1, D), ln1_b[l].reshape(1, D), alpha=alpha)
        x2 = _ffn_ln(h, w_ff1_t, w_ff2_b, ln2_g[l].reshape(1, D), ln2_b[l].reshape(1, D), alpha=alpha)
    return x2.reshape(B, S, D)
```

```python
import functools

import jax
import jax.numpy as jnp
from jax import lax
from jax.experimental import pallas as pl
from jax.experimental.pallas import tpu as pltpu

HEAD_DIM = 128
POOL_WINDOWS = (2, 4, 8, 16)
MOBA_BLOCK = 256
MOBA_TOPK = 3
MOBA_GROUP = 4
MOBA_HEADS_PER_STEP = 2
LOG2_E = 1.4426950408889634
FFN_HIDDEN_TILE = 512
OUTPROJ_COL_TILE = 1024
ROPE_THETA = 10000.0
LN_EPS = 1e-5
POOL_HALO = 16

V7X_VMEM_LIMIT_BYTES = 60 * 1024 * 1024

F32 = jnp.float32
BF16 = jnp.bfloat16


def _pick(n, pref):
    t = min(pref, n)
    while n % t:
        t //= 2
    return t


def _params(*sem):
    return pltpu.CompilerParams(dimension_semantics=sem, vmem_limit_bytes=V7X_VMEM_LIMIT_BYTES)


def _proj_pool_kernel(x_ref, w_ref, wp_ref, scale_ref, pool_ref, xb_ref, ext_ref, *, tiles_per_seq):
    tm = x_ref.shape[0]
    cg = wp_ref.shape[1]
    s = pl.program_id(0) % tiles_per_seq
    xb = x_ref[...].astype(xb_ref.dtype)
    xb_ref[...] = xb

    @pl.when(s == 0)
    def _():
        ext_ref[0:POOL_HALO, :] = jnp.zeros((POOL_HALO, ext_ref.shape[1]), F32)

    @pl.when(s != 0)
    def _():
        ext_ref[0:POOL_HALO, :] = ext_ref[tm:tm + POOL_HALO, :]

    pos1 = lax.broadcasted_iota(jnp.int32, (tm, cg), 0) + (s * tm + 1)
    proj = lambda g: jnp.dot(xb, w_ref[:, g * cg:(g + 1) * cg], preferred_element_type=F32)
    order = sorted(range(len(POOL_WINDOWS)), key=lambda g: -POOL_WINDOWS[g])
    u_next = proj(order[0])
    for k, g in enumerate(order):
        w = POOL_WINDOWS[g]
        cs = slice(g * cg, (g + 1) * cg)
        ug = u_next
        if k + 1 < len(order):
            u_next = proj(order[k + 1])
        ext_ref[POOL_HALO:, cs] = ug
        tot = ug
        for d in range(1, w):
            tot = tot + ext_ref[POOL_HALO - d:POOL_HALO - d + tm, cs]
        count = jnp.minimum(pos1, w).astype(F32)
        pooled = tot / count - ug
        out = jnp.dot(pooled.astype(BF16), wp_ref[g], preferred_element_type=F32) * scale_ref[:, cs]
        pool_ref[:, cs] = out.astype(pool_ref.dtype)


def _proj_heads_kernel(*refs, rope):
    if rope:
        x_ref, w_ref, cos_ref, sin_ref, ci_ref, o_ref, co_ref = refs
    else:
        x_ref, w_ref, ci_ref, o_ref, co_ref = refs
    _cast_slab(ci_ref, co_ref)
    acc = jnp.dot(x_ref[...], w_ref[...], preferred_element_type=F32)
    for hh in range(o_ref.shape[0]):
        t = acc[:, hh * HEAD_DIM:(hh + 1) * HEAD_DIM]
        if rope:
            t = t * cos_ref[...] + pltpu.roll(t, HEAD_DIM // 2, 1) * sin_ref[...]
        o_ref[hh] = t.astype(o_ref.dtype)


def _cast_stream(w_f32, grid, col_tile=None):
    R, C = w_f32.shape
    steps = grid[0] * grid[1]
    rb = R // steps
    assert R % steps == 0 and rb % 16 == 0, (R, steps)
    step = lambda i, j: i * grid[1] + j
    in_spec = pl.BlockSpec((rb, C), lambda i, j: (step(i, j), 0))
    if col_tile is None:
        return in_spec, in_spec, jax.ShapeDtypeStruct((R, C), BF16)
    assert C % col_tile == 0
    out_spec = pl.BlockSpec((C // col_tile, rb, col_tile), lambda i, j: (0, step(i, j), 0))
    return in_spec, out_spec, jax.ShapeDtypeStruct((C // col_tile, R, col_tile), BF16)


def _cast_slab(ci_ref, co_ref):
    if len(co_ref.shape) == 2:
        co_ref[...] = ci_ref[...].astype(co_ref.dtype)
    else:
        ct = co_ref.shape[2]
        for c in range(co_ref.shape[0]):
            co_ref[c] = ci_ref[:, c * ct:(c + 1) * ct].astype(co_ref.dtype)


def _project_pool(x2, w, w_pool, pool_scale, *, seq, col0):
    T, D = x2.shape
    G, cg, _ = w_pool.shape
    P = G * cg
    tm = _pick(seq, 512)
    assert tm >= POOL_HALO >= max(POOL_WINDOWS) - 1 and col0 % P == 0
    once = dict(pipeline_mode=pl.Buffered(1))
    return pl.pallas_call(
        functools.partial(_proj_pool_kernel, tiles_per_seq=seq // tm),
        grid=(T // tm,),
        in_specs=[
            pl.BlockSpec((tm, D), lambda i: (i, 0)),
            pl.BlockSpec((D, P), lambda i: (0, col0 // P), **once),
            pl.BlockSpec((G, cg, cg), lambda i: (0, 0, 0), **once),
            pl.BlockSpec((1, P), lambda i: (0, 0)),
        ],
        out_specs=[pl.BlockSpec((tm, P), lambda i: (i, 0)), pl.BlockSpec((tm, D), lambda i: (i, 0))],
        out_shape=[jax.ShapeDtypeStruct((T, P), BF16), jax.ShapeDtypeStruct((T, D), BF16)],
        scratch_shapes=[pltpu.VMEM((POOL_HALO + tm, P), F32)],
        compiler_params=_params("arbitrary"),
        name="proj_pool",
    )(x2, w, w_pool, pool_scale)


def _project_heads(xb, w, w_cast, *, batch, seq, col0, ncols, cast_col_tile, rope_tables=None):
    T, D = xb.shape
    tm = _pick(seq, 1024)
    tn = _pick(ncols // 2, 1024)
    spt = seq // tm
    hpt = tn // HEAD_DIM
    j0 = col0 // tn
    grid = (T // tm, ncols // tn)
    half = grid[1] // 2
    cast_in, cast_out, cast_shape = _cast_stream(w_cast, grid, cast_col_tile)
    table_spec = pl.BlockSpec((None, tm, HEAD_DIM), lambda i, j: (j // half, i % spt, 0))
    tables = () if rope_tables is None else tuple(rope_tables)
    return pl.pallas_call(
        functools.partial(_proj_heads_kernel, rope=rope_tables is not None),
        grid=grid,
        in_specs=[
            pl.BlockSpec((tm, D), lambda i, j: (i, 0)),
            pl.BlockSpec((D, tn), lambda i, j: (0, j + j0)),
            *([table_spec] * len(tables)),
            cast_in,
        ],
        out_specs=[
            pl.BlockSpec((None, hpt, tm, HEAD_DIM), lambda i, j: (i // spt, j, i % spt, 0)),
            cast_out,
        ],
        out_shape=[jax.ShapeDtypeStruct((batch, ncols // HEAD_DIM, seq, HEAD_DIM), BF16), cast_shape],
        compiler_params=_params("parallel", "arbitrary"),
        name="proj_heads" if rope_tables is None else "proj_rope_heads",
    )(xb, w, *tables, w_cast)


def _moba_kernel(q_ref, k_ref, v_ref, ci_ref, o_ref, co_ref, vtg_ref, kmean_ref, bias_ref):
    blk, grp = MOBA_BLOCK, MOBA_GROUP
    quad = grp * blk
    hps, seq, _ = k_ref.shape
    nb = seq // blk
    ngrp = nb // grp
    nt = (((1,), (1,)), ((), ()))

    _cast_slab(ci_ref, co_ref)

    def setup(h):
        for j in range(nb):
            kj = k_ref[h, j * blk:(j + 1) * blk, :].astype(F32)
            kmean_ref[h, j:j + 1, :] = jnp.sum(kj, axis=0, keepdims=True) * (1.0 / blk)
            vt = v_ref[h, j * blk:(j + 1) * blk, :].astype(F32).T.astype(BF16)
            vtg_ref[h, j // grp, 0:HEAD_DIM, (j % grp) * blk:(j % grp + 1) * blk] = vt
        ones_row = lax.broadcasted_iota(jnp.int32, (vtg_ref.shape[2] - HEAD_DIM, quad), 0) == 0
        for g in range(ngrp):
            vtg_ref[h, g, HEAD_DIM:, :] = jnp.where(ones_row, 1.0, 0.0).astype(BF16)
        gate = lax.dot_general(kmean_ref[h].astype(BF16), q_ref[h], nt, preferred_element_type=F32)
        row = lax.broadcasted_iota(jnp.int32, gate.shape, 0)
        qblk = lax.shift_right_logical(lax.broadcasted_iota(jnp.int32, gate.shape, 1), blk.bit_length() - 1)
        past = row < qblk
        gate = jnp.where(past, gate, -jnp.inf)
        rank = jnp.zeros(gate.shape, jnp.int32)
        for jp in range(nb):
            gj = gate[jp:jp + 1, :]
            beats = (gj > gate) | ((gj == gate) & (jp < row))
            rank = rank + beats.astype(jnp.int32)
        bias = jnp.where(past & (rank < MOBA_TOPK), 0.0, -jnp.inf).astype(F32)
        for g in range(ngrp):
            for qd in range(g, ngrp):
                bias_ref[h, g, qd, 0:grp, :] = bias[g * grp:(g + 1) * grp, qd * quad:(qd + 1) * quad]

    for h in range(hps):
        setup(h)

    kpos = lax.broadcasted_iota(jnp.int32, (blk, blk), 0)
    qpos = lax.broadcasted_iota(jnp.int32, (blk, blk), 1)
    causal = kpos <= qpos

    def scores(k_rows, q):
        return lax.dot_general(k_rows, q, nt, preferred_element_type=F32)

    def tiles_max(tiles):
        m = jnp.max(tiles[0], axis=0, keepdims=True)
        for t in tiles[1:]:
            m = jnp.maximum(m, jnp.max(t, axis=0, keepdims=True))
        return m

    heads = range(hps)

    def diagonal_group(qd, q0):
        s = [[scores(k_ref[h, pl.ds(q0, (r + 1) * blk), :],
                     q_ref[h, pl.ds(pl.multiple_of(q0 + r * blk, blk), blk), :]) for r in range(grp)]
             for h in heads]
        m, p = [], []
        for h in heads:
            mr, pr = [], []
            for r in range(grp):
                brow = bias_ref[h, qd, qd, 0:grp, r * blk:(r + 1) * blk]
                tiles = [s[h][r][t * blk:(t + 1) * blk] + brow[t:t + 1, :] for t in range(r)]
                tiles.append(jnp.where(causal, s[h][r][r * blk:(r + 1) * blk], -jnp.inf))
                mr.append(tiles_max(tiles))
                pr.append(jnp.concatenate([jnp.exp2(t - mr[r]) for t in tiles], axis=0).astype(BF16))
            m.append(jnp.concatenate(mr, axis=1))
            p.append(pr)
        acc = [jnp.concatenate(
            [jnp.dot(vtg_ref[h, qd, :, 0:(r + 1) * blk], p[h][r], preferred_element_type=F32)
             for r in range(grp)], axis=1) for h in heads]
        return tuple((m[h], acc[h]) for h in heads)

    def past_group(qd, q0, g, state):
        k0 = pl.multiple_of(g * quad, quad)
        s = [scores(k_ref[h, pl.ds(k0, quad), :], q_ref[h, pl.ds(q0, quad), :]) for h in heads]
        stats, p = [], []
        for h in heads:
            m = state[h][0]
            brow = bias_ref[h, g, qd, 0:grp, :]
            tiles = [s[h][t * blk:(t + 1) * blk] + brow[t:t + 1, :] for t in range(grp)]
            m_new = jnp.maximum(m, tiles_max(tiles))
            p.append(jnp.concatenate([jnp.exp2(t - m_new) for t in tiles], axis=0).astype(BF16))
            stats.append((m_new, jnp.exp2(m - m_new)))
        pv = [jnp.dot(vtg_ref[h, g], p[h], preferred_element_type=F32) for h in heads]
        return tuple((stats[h][0], stats[h][1] * state[h][1] + pv[h]) for h in heads)

    def query_quad(qd, _):
        q0 = pl.multiple_of(qd * quad, quad)
        state = lax.fori_loop(0, qd, functools.partial(past_group, qd, q0), diagonal_group(qd, q0))
        for h, (_, acc) in enumerate(state):
            out = acc[0:HEAD_DIM] / acc[HEAD_DIM:HEAD_DIM + 1]
            o_ref[pl.ds(q0, quad), h * HEAD_DIM:(h + 1) * HEAD_DIM] = out.T.astype(o_ref.dtype)
        return 0

    lax.fori_loop(0, ngrp, query_quad, 0)


def _moba_attention(qk, v, w_cast, *, seq, n_heads):
    batch = qk.shape[0]
    blk, grp = MOBA_BLOCK, MOBA_GROUP
    hps = _pick(n_heads, MOBA_HEADS_PER_STEP)
    nb = seq // blk
    assert seq % (blk * grp) == 0 and blk & (blk - 1) == 0 and grp & (grp - 1) == 0
    heads = lambda off: pl.BlockSpec((None, hps, seq, HEAD_DIM), lambda b, h: (b, h + off, 0, 0))
    grid = (batch, n_heads // hps)
    cast_in, cast_out, cast_shape = _cast_stream(w_cast, grid)
    return pl.pallas_call(
        _moba_kernel,
        grid=grid,
        in_specs=[heads(0), heads(n_heads // hps), heads(0), cast_in],
        out_specs=[pl.BlockSpec((None, seq, hps * HEAD_DIM), lambda b, h: (b, 0, h)), cast_out],
        out_shape=[jax.ShapeDtypeStruct((batch, seq, n_heads * HEAD_DIM), BF16), cast_shape],
        scratch_shapes=[
            pltpu.VMEM((hps, nb // grp, HEAD_DIM + 16, grp * blk), BF16),
            pltpu.VMEM((hps, nb, HEAD_DIM), F32),
            pltpu.VMEM((hps, nb // grp, nb // grp, 8, grp * blk), F32),
        ],
        compiler_params=_params("parallel", "parallel"),
        name="moba_attention",
    )(qk, qk, v, w_cast)


def _layer_norm_tiles(tile, nj, tn, g_ref, b_ref, o_ref):
    d = nj * tn
    tot = jnp.sum(tile(0), axis=1, keepdims=True)
    for jj in range(1, nj):
        tot = tot + jnp.sum(tile(jj), axis=1, keepdims=True)
    mu = tot * (1.0 / d)
    sq = jnp.sum(jnp.square(tile(0) - mu), axis=1, keepdims=True)
    for jj in range(1, nj):
        sq = sq + jnp.sum(jnp.square(tile(jj) - mu), axis=1, keepdims=True)
    rstd = lax.rsqrt(sq * (1.0 / d) + LN_EPS)
    for jj in range(nj):
        cs = slice(jj * tn, (jj + 1) * tn)
        o_ref[:, cs] = (tile(jj) - mu) * rstd * g_ref[:, cs] + b_ref[:, cs]


def _outproj_ln_kernel(a_ref, p_ref, w_ref, x_ref, g_ref, b_ref, o_ref, *, alpha, tn):
    ka = a_ref.shape[1]
    nj = o_ref.shape[1] // tn
    for jj in range(nj):
        cs = slice(jj * tn, (jj + 1) * tn)
        mix = jnp.dot(a_ref[...], w_ref[0:ka, cs], preferred_element_type=F32)
        mix = mix + jnp.dot(p_ref[...], w_ref[ka:, cs], preferred_element_type=F32)
        o_ref[:, cs] = alpha * x_ref[:, cs] + mix
    _layer_norm_tiles(lambda jj: o_ref[:, jj * tn:(jj + 1) * tn], nj, tn, g_ref, b_ref, o_ref)


def _outproj_ln(attn, pool, w_out, x2, g, b, *, alpha):
    T, D = x2.shape
    ka, kp = attn.shape[1], pool.shape[1]
    tm = _pick(T, 256)
    return pl.pallas_call(
        functools.partial(_outproj_ln_kernel, alpha=alpha, tn=_pick(D, OUTPROJ_COL_TILE)),
        grid=(T // tm,),
        in_specs=[
            pl.BlockSpec((tm, ka), lambda i: (i, 0)),
            pl.BlockSpec((tm, kp), lambda i: (i, 0)),
            pl.BlockSpec((ka + kp, D), lambda i: (0, 0), pipeline_mode=pl.Buffered(1)),
            pl.BlockSpec((tm, D), lambda i: (i, 0)),
            pl.BlockSpec((1, D), lambda i: (0, 0)),
            pl.BlockSpec((1, D), lambda i: (0, 0)),
        ],
        out_specs=pl.BlockSpec((tm, D), lambda i: (i, 0)),
        out_shape=jax.ShapeDtypeStruct((T, D), F32),
        compiler_params=_params("parallel"),
        name="outproj_ln1",
    )(attn, pool, w_out, x2, g, b)


def _ffn_ln_kernel(h_ref, w1_ref, w2_ref, g_ref, b_ref, o_ref, hb_ref, *, alpha, tn):
    f = pl.program_id(1)
    nj = o_ref.shape[1] // tn

    @pl.when(f == 0)
    def _():
        hb_ref[...] = h_ref[...].astype(BF16)
        o_ref[...] = jnp.zeros_like(o_ref)

    hid = jnp.dot(hb_ref[...], w1_ref[...], preferred_element_type=F32)
    hid = jnp.square(jnp.maximum(hid, 0.0)).astype(BF16)
    for jj in range(nj):
        cs = slice(jj * tn, (jj + 1) * tn)
        o_ref[:, cs] += jnp.dot(hid, w2_ref[:, cs], preferred_element_type=F32)

    @pl.when(f == pl.num_programs(1) - 1)
    def _():
        o_ref[...] = alpha * h_ref[...] + o_ref[...]
        _layer_norm_tiles(lambda jj: o_ref[:, jj * tn:(jj + 1) * tn], nj, tn, g_ref, b_ref, o_ref)


def _ffn_ln(h, w1_t, w2, g, b, *, alpha):
    T, D = h.shape
    nf, _, tf = w1_t.shape
    tm = _pick(T, 512)
    tn = _pick(D, 512)
    return pl.pallas_call(
        functools.partial(_ffn_ln_kernel, alpha=alpha, tn=tn),
        grid=(T // tm, nf),
        in_specs=[
            pl.BlockSpec((tm, D), lambda i, f: (i, 0), pipeline_mode=pl.Buffered(1)),
            pl.BlockSpec((None, D, tf), lambda i, f: (f, 0, 0)),
            pl.BlockSpec((tf, D), lambda i, f: (f, 0)),
            pl.BlockSpec((1, D), lambda i, f: (0, 0)),
            pl.BlockSpec((1, D), lambda i, f: (0, 0)),
        ],
        out_specs=pl.BlockSpec((tm, D), lambda i, f: (i, 0)),
        out_shape=jax.ShapeDtypeStruct((T, D), F32),
        scratch_shapes=[pltpu.VMEM((tm, D), BF16)],
        compiler_params=_params("parallel", "arbitrary"),
        name="ffn_ln2",
    )(h, w1_t, w2, g, b)


def _rope_tables(seq, q_scale):
    inv_freq = 1.0 / (ROPE_THETA ** (jnp.arange(0, HEAD_DIM, 2, dtype=F32) / HEAD_DIM))
    ang = jnp.arange(seq, dtype=F32)[:, None] * inv_freq[None, :]
    cos, sin = jnp.cos(ang), jnp.sin(ang)
    cos = jnp.concatenate([cos, cos], axis=-1)
    sin = jnp.concatenate([-sin, sin], axis=-1)
    return jnp.stack([q_scale * cos, cos]), jnp.stack([q_scale * sin, sin])


def kernel(x, w_in, w_pool, pool_scale, w_out, ln1_g, ln1_b, w_ff1, w_ff2, ln2_g, ln2_b):
    B, S, D = x.shape
    depth = w_in.shape[0]
    attn_w = D // 2
    n_heads = attn_w // HEAD_DIM
    pool_w = D - attn_w
    alpha = (2.0 * depth) ** 0.25
    rope_tables = _rope_tables(S, HEAD_DIM ** -0.5 * LOG2_E)

    x2 = x.reshape(B * S, D)
    for l in range(depth):
        w_in_b = w_in[l].astype(BF16)
        pool, xb = _project_pool(x2, w_in_b, w_pool[l].astype(BF16), pool_scale[l].reshape(1, pool_w),
                                 seq=S, col0=3 * attn_w)
        qk, w_ff1_t = _project_heads(xb, w_in_b, w_ff1[l], batch=B, seq=S, col0=0, ncols=2 * attn_w,
                                     cast_col_tile=_pick(w_ff1.shape[2], FFN_HIDDEN_TILE),
                                     rope_tables=rope_tables)
        v, w_out_b = _project_heads(xb, w_in_b, w_out[l], batch=B, seq=S, col0=2 * attn_w, ncols=attn_w,
                                    cast_col_tile=None)

        attn, w_ff2_b = _moba_attention(qk, v, w_ff2[l], seq=S, n_heads=n_heads)
        h = _outproj_ln(attn.reshape(B * S, attn_w), pool, w_out_b, x2,
                        ln1_g[l].reshape(1, D), ln1_b[l].reshape(1, D), alpha=alpha)
        x2 = _ffn_ln(h, w_ff1_t, w_ff2_b, ln2_g[l].reshape(1, D), ln2_b[l].reshape(1, D), alpha=alpha)
    return x2.reshape(B, S, D)
```

```python
import functools

import jax
import jax.numpy as jnp
from jax import lax
from jax.experimental import pallas as pl
from jax.experimental.pallas import tpu as pltpu

HEAD_DIM = 128
POOL_WINDOWS = (2, 4, 8, 16)
MOBA_BLOCK = 256
MOBA_TOPK = 3
MOBA_GROUP = 4
MOBA_HEADS_PER_STEP = 2
LOG2_E = 1.4426950408889634
FFN_HIDDEN_TILE = 512
OUTPROJ_COL_TILE = 1024
ROPE_THETA = 10000.0
LN_EPS = 1e-5
POOL_HALO = 16

V7X_VMEM_LIMIT_BYTES = 62 * 1024 * 1024

F32 = jnp.float32
BF16 = jnp.bfloat16


def _pick(n, pref):
    t = min(pref, n)
    while n % t:
        t //= 2
    return t


def _params(*sem):
    return pltpu.CompilerParams(dimension_semantics=sem, vmem_limit_bytes=V7X_VMEM_LIMIT_BYTES)


def _proj_pool_kernel(x_ref, w_ref, wp_ref, scale_ref, pool_ref, xb_ref, ext_ref, *, tiles_per_seq):
    tm = x_ref.shape[0]
    cg = wp_ref.shape[1]
    s = pl.program_id(0) % tiles_per_seq
    xb = x_ref[...].astype(xb_ref.dtype)
    xb_ref[...] = xb

    @pl.when(s == 0)
    def _():
        ext_ref[0:POOL_HALO, :] = jnp.zeros((POOL_HALO, ext_ref.shape[1]), F32)

    @pl.when(s != 0)
    def _():
        ext_ref[0:POOL_HALO, :] = ext_ref[tm:tm + POOL_HALO, :]

    pos1 = lax.broadcasted_iota(jnp.int32, (tm, cg), 0) + (s * tm + 1)
    proj = lambda g: jnp.dot(xb, w_ref[:, g * cg:(g + 1) * cg], preferred_element_type=F32)
    order = sorted(range(len(POOL_WINDOWS)), key=lambda g: -POOL_WINDOWS[g])
    u_next = proj(order[0])
    for k, g in enumerate(order):
        w = POOL_WINDOWS[g]
        cs = slice(g * cg, (g + 1) * cg)
        ug = u_next
        if k + 1 < len(order):
            u_next = proj(order[k + 1])
        ext_ref[POOL_HALO:, cs] = ug
        tot = ug
        for d in range(1, w):
            tot = tot + ext_ref[POOL_HALO - d:POOL_HALO - d + tm, cs]
        count = jnp.minimum(pos1, w).astype(F32)
        pooled = tot / count - ug
        out = jnp.dot(pooled.astype(BF16), wp_ref[g], preferred_element_type=F32) * scale_ref[:, cs]
        pool_ref[:, cs] = out.astype(pool_ref.dtype)


def _proj_heads_kernel(*refs, rope):
    if rope:
        x_ref, w_ref, cos_ref, sin_ref, ci_ref, o_ref, co_ref = refs
    else:
        x_ref, w_ref, ci_ref, o_ref, co_ref = refs
    _cast_slab(ci_ref, co_ref)
    acc = jnp.dot(x_ref[...], w_ref[...], preferred_element_type=F32)
    for hh in range(o_ref.shape[0]):
        t = acc[:, hh * HEAD_DIM:(hh + 1) * HEAD_DIM]
        if rope:
            t = t * cos_ref[...] + pltpu.roll(t, HEAD_DIM // 2, 1) * sin_ref[...]
        o_ref[hh] = t.astype(o_ref.dtype)


def _cast_stream(w_f32, grid, col_tile=None):
    R, C = w_f32.shape
    steps = grid[0] * grid[1]
    rb = R // steps
    assert R % steps == 0 and rb % 16 == 0, (R, steps)
    step = lambda i, j: i * grid[1] + j
    in_spec = pl.BlockSpec((rb, C), lambda i, j: (step(i, j), 0))
    if col_tile is None:
        return in_spec, in_spec, jax.ShapeDtypeStruct((R, C), BF16)
    assert C % col_tile == 0
    out_spec = pl.BlockSpec((C // col_tile, rb, col_tile), lambda i, j: (0, step(i, j), 0))
    return in_spec, out_spec, jax.ShapeDtypeStruct((C // col_tile, R, col_tile), BF16)


def _cast_slab(ci_ref, co_ref):
    if len(co_ref.shape) == 2:
        co_ref[...] = ci_ref[...].astype(co_ref.dtype)
    else:
        ct = co_ref.shape[2]
        for c in range(co_ref.shape[0]):
            co_ref[c] = ci_ref[:, c * ct:(c + 1) * ct].astype(co_ref.dtype)


def _project_pool(x2, w, w_pool, pool_scale, *, seq, col0):
    T, D = x2.shape
    G, cg, _ = w_pool.shape
    P = G * cg
    tm = _pick(seq, 512)
    assert tm >= POOL_HALO >= max(POOL_WINDOWS) - 1 and col0 % P == 0
    once = dict(pipeline_mode=pl.Buffered(1))
    return pl.pallas_call(
        functools.partial(_proj_pool_kernel, tiles_per_seq=seq // tm),
        grid=(T // tm,),
        in_specs=[
            pl.BlockSpec((tm, D), lambda i: (i, 0)),
            pl.BlockSpec((D, P), lambda i: (0, col0 // P), **once),
            pl.BlockSpec((G, cg, cg), lambda i: (0, 0, 0), **once),
            pl.BlockSpec((1, P), lambda i: (0, 0)),
        ],
        out_specs=[pl.BlockSpec((tm, P), lambda i: (i, 0)), pl.BlockSpec((tm, D), lambda i: (i, 0))],
        out_shape=[jax.ShapeDtypeStruct((T, P), BF16), jax.ShapeDtypeStruct((T, D), BF16)],
        scratch_shapes=[pltpu.VMEM((POOL_HALO + tm, P), F32)],
        compiler_params=_params("arbitrary"),
        name="proj_pool",
    )(x2, w, w_pool, pool_scale)


def _project_heads(xb, w, w_cast, *, batch, seq, col0, ncols, cast_col_tile, rope_tables=None):
    T, D = xb.shape
    tm = _pick(seq, 1024)
    tn = _pick(ncols // 2, 1024)
    spt = seq // tm
    hpt = tn // HEAD_DIM
    j0 = col0 // tn
    grid = (T // tm, ncols // tn)
    half = grid[1] // 2
    cast_in, cast_out, cast_shape = _cast_stream(w_cast, grid, cast_col_tile)
    table_spec = pl.BlockSpec((None, tm, HEAD_DIM), lambda i, j: (j // half, i % spt, 0))
    tables = () if rope_tables is None else tuple(rope_tables)
    return pl.pallas_call(
        functools.partial(_proj_heads_kernel, rope=rope_tables is not None),
        grid=grid,
        in_specs=[
            pl.BlockSpec((tm, D), lambda i, j: (i, 0)),
            pl.BlockSpec((D, tn), lambda i, j: (0, j + j0)),
            *([table_spec] * len(tables)),
            cast_in,
        ],
        out_specs=[
            pl.BlockSpec((None, hpt, tm, HEAD_DIM), lambda i, j: (i // spt, j, i % spt, 0)),
            cast_out,
        ],
        out_shape=[jax.ShapeDtypeStruct((batch, ncols // HEAD_DIM, seq, HEAD_DIM), BF16), cast_shape],
        compiler_params=_params("parallel", "arbitrary"),
        name="proj_heads" if rope_tables is None else "proj_rope_heads",
    )(xb, w, *tables, w_cast)


def _moba_kernel(q_ref, k_ref, v_ref, ci_ref, o_ref, co_ref, vtg_ref, kmean_ref, bias_ref):
    blk, grp = MOBA_BLOCK, MOBA_GROUP
    quad = grp * blk
    hps, seq, _ = k_ref.shape
    nb = seq // blk
    ngrp = nb // grp
    nt = (((1,), (1,)), ((), ()))

    _cast_slab(ci_ref, co_ref)

    def setup(h):
        for j in range(nb):
            kj = k_ref[h, j * blk:(j + 1) * blk, :].astype(F32)
            kmean_ref[h, j:j + 1, :] = jnp.sum(kj, axis=0, keepdims=True) * (1.0 / blk)
            vt = v_ref[h, j * blk:(j + 1) * blk, :].astype(F32).T.astype(BF16)
            vtg_ref[h, j // grp, 0:HEAD_DIM, (j % grp) * blk:(j % grp + 1) * blk] = vt
        ones_row = lax.broadcasted_iota(jnp.int32, (vtg_ref.shape[2] - HEAD_DIM, quad), 0) == 0
        for g in range(ngrp):
            vtg_ref[h, g, HEAD_DIM:, :] = jnp.where(ones_row, 1.0, 0.0).astype(BF16)
        gate = lax.dot_general(kmean_ref[h].astype(BF16), q_ref[h], nt, preferred_element_type=F32)
        row = lax.broadcasted_iota(jnp.int32, gate.shape, 0)
        qblk = lax.shift_right_logical(lax.broadcasted_iota(jnp.int32, gate.shape, 1), blk.bit_length() - 1)
        past = row < qblk
        gate = jnp.where(past, gate, -jnp.inf)
        rank = jnp.zeros(gate.shape, jnp.int32)
        for jp in range(nb):
            gj = gate[jp:jp + 1, :]
            beats = (gj > gate) | ((gj == gate) & (jp < row))
            rank = rank + beats.astype(jnp.int32)
        bias = jnp.where(past & (rank < MOBA_TOPK), 0.0, -jnp.inf).astype(F32)
        for g in range(ngrp):
            for qd in range(g, ngrp):
                bias_ref[h, g, qd, 0:grp, :] = bias[g * grp:(g + 1) * grp, qd * quad:(qd + 1) * quad]

    for h in range(hps):
        setup(h)

    kpos = lax.broadcasted_iota(jnp.int32, (blk, blk), 0)
    qpos = lax.broadcasted_iota(jnp.int32, (blk, blk), 1)
    causal = kpos <= qpos

    def scores(k_rows, q):
        return lax.dot_general(k_rows, q, nt, preferred_element_type=F32)

    def tiles_max(tiles):
        m = jnp.max(tiles[0], axis=0, keepdims=True)
        for t in tiles[1:]:
            m = jnp.maximum(m, jnp.max(t, axis=0, keepdims=True))
        return m

    heads = range(hps)

    def diagonal_group(qd, q0):
        s = [[scores(k_ref[h, pl.ds(q0, (r + 1) * blk), :],
                     q_ref[h, pl.ds(pl.multiple_of(q0 + r * blk, blk), blk), :]) for r in range(grp)]
             for h in heads]
        m, p = [], []
        for h in heads:
            mr, pr = [], []
            for r in range(grp):
                brow = bias_ref[h, qd, qd, 0:grp, r * blk:(r + 1) * blk]
                tiles = [s[h][r][t * blk:(t + 1) * blk] + brow[t:t + 1, :] for t in range(r)]
                tiles.append(jnp.where(causal, s[h][r][r * blk:(r + 1) * blk], -jnp.inf))
                mr.append(tiles_max(tiles))
                pr.append(jnp.concatenate([jnp.exp2(t - mr[r]) for t in tiles], axis=0).astype(BF16))
            m.append(jnp.concatenate(mr, axis=1))
            p.append(pr)
        acc = [jnp.concatenate(
            [jnp.dot(vtg_ref[h, qd, :, 0:(r + 1) * blk], p[h][r], preferred_element_type=F32)
             for r in range(grp)], axis=1) for h in heads]
        return tuple((m[h], acc[h]) for h in heads)

    def past_group(qd, q0, g, state):
        k0 = pl.multiple_of(g * quad, quad)
        s = [scores(k_ref[h, pl.ds(k0, quad), :], q_ref[h, pl.ds(q0, quad), :]) for h in heads]
        stats, p = [], []
        for h in heads:
            m = state[h][0]
            brow = bias_ref[h, g, qd, 0:grp, :]
            tiles = [s[h][t * blk:(t + 1) * blk] + brow[t:t + 1, :] for t in range(grp)]
            m_new = jnp.maximum(m, tiles_max(tiles))
            p.append(jnp.concatenate([jnp.exp2(t - m_new) for t in tiles], axis=0).astype(BF16))
            stats.append((m_new, jnp.exp2(m - m_new)))
        pv = [jnp.dot(vtg_ref[h, g], p[h], preferred_element_type=F32) for h in heads]
        return tuple((stats[h][0], stats[h][1] * state[h][1] + pv[h]) for h in heads)

    def query_quad(qd, _):
        q0 = pl.multiple_of(qd * quad, quad)
        state = lax.fori_loop(0, qd, functools.partial(past_group, qd, q0), diagonal_group(qd, q0))
        for h, (_, acc) in enumerate(state):
            out = acc[0:HEAD_DIM] / acc[HEAD_DIM:HEAD_DIM + 1]
            o_ref[pl.ds(q0, quad), h * HEAD_DIM:(h + 1) * HEAD_DIM] = out.T.astype(o_ref.dtype)
        return 0

    lax.fori_loop(0, ngrp, query_quad, 0)


def _moba_attention(qk, v, w_cast, *, seq, n_heads):
    batch = qk.shape[0]
    blk, grp = MOBA_BLOCK, MOBA_GROUP
    hps = _pick(n_heads, MOBA_HEADS_PER_STEP)
    nb = seq // blk
    assert seq % (blk * grp) == 0 and blk & (blk - 1) == 0 and grp & (grp - 1) == 0
    heads = lambda off: pl.BlockSpec((None, hps, seq, HEAD_DIM), lambda b, h: (b, h + off, 0, 0))
    grid = (batch, n_heads // hps)
    cast_in, cast_out, cast_shape = _cast_stream(w_cast, grid)
    return pl.pallas_call(
        _moba_kernel,
        grid=grid,
        in_specs=[heads(0), heads(n_heads // hps), heads(0), cast_in],
        out_specs=[pl.BlockSpec((None, seq, hps * HEAD_DIM), lambda b, h: (b, 0, h)), cast_out],
        out_shape=[jax.ShapeDtypeStruct((batch, seq, n_heads * HEAD_DIM), BF16), cast_shape],
        scratch_shapes=[
            pltpu.VMEM((hps, nb // grp, HEAD_DIM + 16, grp * blk), BF16),
            pltpu.VMEM((hps, nb, HEAD_DIM), F32),
            pltpu.VMEM((hps, nb // grp, nb // grp, 8, grp * blk), F32),
        ],
        compiler_params=_params("parallel", "parallel"),
        name="moba_attention",
    )(qk, qk, v, w_cast)


def _layer_norm_tiles(tile, nj, tn, g_ref, b_ref, o_ref, ob_ref=None):
    d = nj * tn
    tot = jnp.sum(tile(0), axis=1, keepdims=True)
    for jj in range(1, nj):
        tot = tot + jnp.sum(tile(jj), axis=1, keepdims=True)
    mu = tot * (1.0 / d)
    sq = jnp.sum(jnp.square(tile(0) - mu), axis=1, keepdims=True)
    for jj in range(1, nj):
        sq = sq + jnp.sum(jnp.square(tile(jj) - mu), axis=1, keepdims=True)
    rstd = lax.rsqrt(sq * (1.0 / d) + LN_EPS)
    for jj in range(nj):
        cs = slice(jj * tn, (jj + 1) * tn)
        y = (tile(jj) - mu) * rstd * g_ref[:, cs] + b_ref[:, cs]
        o_ref[:, cs] = y
        if ob_ref is not None:
            ob_ref[:, cs] = y.astype(ob_ref.dtype)


def _outproj_ln_kernel(a_ref, p_ref, w_ref, x_ref, g_ref, b_ref, o_ref, ob_ref, *, alpha, tn):
    ka = a_ref.shape[1]
    nj = o_ref.shape[1] // tn
    for jj in range(nj):
        cs = slice(jj * tn, (jj + 1) * tn)
        mix = jnp.dot(a_ref[...], w_ref[0:ka, cs], preferred_element_type=F32)
        mix = mix + jnp.dot(p_ref[...], w_ref[ka:, cs], preferred_element_type=F32)
        o_ref[:, cs] = alpha * x_ref[:, cs] + mix
    _layer_norm_tiles(lambda jj: o_ref[:, jj * tn:(jj + 1) * tn], nj, tn, g_ref, b_ref, o_ref, ob_ref)


def _outproj_ln(attn, pool, w_out, x2, g, b, *, alpha):
    T, D = x2.shape
    ka, kp = attn.shape[1], pool.shape[1]
    tm = _pick(T, 256)
    return pl.pallas_call(
        functools.partial(_outproj_ln_kernel, alpha=alpha, tn=_pick(D, OUTPROJ_COL_TILE)),
        grid=(T // tm,),
        in_specs=[
            pl.BlockSpec((tm, ka), lambda i: (i, 0)),
            pl.BlockSpec((tm, kp), lambda i: (i, 0)),
            pl.BlockSpec((ka + kp, D), lambda i: (0, 0), pipeline_mode=pl.Buffered(1)),
            pl.BlockSpec((tm, D), lambda i: (i, 0)),
            pl.BlockSpec((1, D), lambda i: (0, 0)),
            pl.BlockSpec((1, D), lambda i: (0, 0)),
        ],
        out_specs=[pl.BlockSpec((tm, D), lambda i: (i, 0))] * 2,
        out_shape=[jax.ShapeDtypeStruct((T, D), F32), jax.ShapeDtypeStruct((T, D), BF16)],
        compiler_params=_params("parallel"),
        name="outproj_ln1",
    )(attn, pool, w_out, x2, g, b)


def _ffn_ln_kernel(hb_ref, h_hbm, w1_ref, w2_ref, g_ref, b_ref, o_ref, res_ref, res_sem, *, alpha, tn):
    f = pl.program_id(1)
    tm = o_ref.shape[0]
    nj = o_ref.shape[1] // tn
    rows = pl.ds(pl.multiple_of(pl.program_id(0) * tm, tm), tm)
    residual_copy = pltpu.make_async_copy(h_hbm.at[rows, :], res_ref, res_sem)

    @pl.when(f == 0)
    def _():
        residual_copy.start()
        o_ref[...] = jnp.zeros_like(o_ref)

    hid = jnp.dot(hb_ref[...], w1_ref[...], preferred_element_type=F32)
    hid = jnp.square(jnp.maximum(hid, 0.0)).astype(BF16)
    for jj in range(nj):
        cs = slice(jj * tn, (jj + 1) * tn)
        o_ref[:, cs] += jnp.dot(hid, w2_ref[:, cs], preferred_element_type=F32)

    @pl.when(f == pl.num_programs(1) - 1)
    def _():
        residual_copy.wait()
        o_ref[...] = alpha * res_ref[...] + o_ref[...]
        _layer_norm_tiles(lambda jj: o_ref[:, jj * tn:(jj + 1) * tn], nj, tn, g_ref, b_ref, o_ref)


def _ffn_ln(hb, h, w1_t, w2, g, b, *, alpha):
    T, D = h.shape
    nf, _, tf = w1_t.shape
    tm = _pick(T, 512)
    tn = _pick(D, 512)
    return pl.pallas_call(
        functools.partial(_ffn_ln_kernel, alpha=alpha, tn=tn),
        grid=(T // tm, nf),
        in_specs=[
            pl.BlockSpec((tm, D), lambda i, f: (i, 0)),
            pl.BlockSpec(memory_space=pl.ANY),
            pl.BlockSpec((None, D, tf), lambda i, f: (f, 0, 0)),
            pl.BlockSpec((tf, D), lambda i, f: (f, 0)),
            pl.BlockSpec((1, D), lambda i, f: (0, 0)),
            pl.BlockSpec((1, D), lambda i, f: (0, 0)),
        ],
        out_specs=pl.BlockSpec((tm, D), lambda i, f: (i, 0)),
        out_shape=jax.ShapeDtypeStruct((T, D), F32),
        scratch_shapes=[pltpu.VMEM((tm, D), F32), pltpu.SemaphoreType.DMA(())],
        compiler_params=_params("parallel", "arbitrary"),
        name="ffn_ln2",
    )(hb, h, w1_t, w2, g, b)


def _rope_tables(seq, q_scale):
    inv_freq = 1.0 / (ROPE_THETA ** (jnp.arange(0, HEAD_DIM, 2, dtype=F32) / HEAD_DIM))
    ang = jnp.arange(seq, dtype=F32)[:, None] * inv_freq[None, :]
    cos, sin = jnp.cos(ang), jnp.sin(ang)
    cos = jnp.concatenate([cos, cos], axis=-1)
    sin = jnp.concatenate([-sin, sin], axis=-1)
    return jnp.stack([q_scale * cos, cos]), jnp.stack([q_scale * sin, sin])


def kernel(x, w_in, w_pool, pool_scale, w_out, ln1_g, ln1_b, w_ff1, w_ff2, ln2_g, ln2_b):
    B, S, D = x.shape
    depth = w_in.shape[0]
    attn_w = D // 2
    n_heads = attn_w // HEAD_DIM
    pool_w = D - attn_w
    alpha = (2.0 * depth) ** 0.25
    rope_tables = _rope_tables(S, HEAD_DIM ** -0.5 * LOG2_E)

    x2 = x.reshape(B * S, D)
    for l in range(depth):
        w_in_b = w_in[l].astype(BF16)
        pool, xb = _project_pool(x2, w_in_b, w_pool[l].astype(BF16), pool_scale[l].reshape(1, pool_w),
                                 seq=S, col0=3 * attn_w)
        qk, w_ff1_t = _project_heads(xb, w_in_b, w_ff1[l], batch=B, seq=S, col0=0, ncols=2 * attn_w,
                                     cast_col_tile=_pick(w_ff1.shape[2], FFN_HIDDEN_TILE),
                                     rope_tables=rope_tables)
        v, w_out_b = _project_heads(xb, w_in_b, w_out[l], batch=B, seq=S, col0=2 * attn_w, ncols=attn_w,
                                    cast_col_tile=None)

        attn, w_ff2_b = _moba_attention(qk, v, w_ff2[l], seq=S, n_heads=n_heads)
        h, hb = _outproj_ln(attn.reshape(B * S, attn_w), pool, w_out_b, x2,
                            ln1_g[l].reshape(1, D), ln1_b[l].reshape(1, D), alpha=alpha)
        x2 = _ffn_ln(hb, h, w_ff1_t, w_ff2_b, ln2_g[l].reshape(1, D), ln2_b[l].reshape(1, D), alpha=alpha)
    return x2.reshape(B, S, D)
```

```python
import functools

import jax
import jax.numpy as jnp
from jax import lax
from jax.experimental import pallas as pl
from jax.experimental.pallas import tpu as pltpu

HEAD_DIM = 128
POOL_WINDOWS = (2, 4, 8, 16)
MOBA_BLOCK = 256
MOBA_TOPK = 3
MOBA_GROUP = 4
MOBA_HEADS_PER_STEP = 2
SOFTMAX_HEADROOM = 64.0
LOG2_E = 1.4426950408889634
FFN_HIDDEN_TILE = 512
OUTPROJ_COL_TILE = 1024
ROPE_THETA = 10000.0
LN_EPS = 1e-5
POOL_HALO = 16

V7X_VMEM_LIMIT_BYTES = 62 * 1024 * 1024

F32 = jnp.float32
BF16 = jnp.bfloat16


def _pick(n, pref):
    t = min(pref, n)
    while n % t:
        t //= 2
    return t


def _params(*sem):
    return pltpu.CompilerParams(dimension_semantics=sem, vmem_limit_bytes=V7X_VMEM_LIMIT_BYTES)


def _proj_pool_kernel(x_ref, w_ref, wp_ref, scale_ref, pool_ref, xb_ref, ext_ref, *, tiles_per_seq):
    tm = x_ref.shape[0]
    cg = wp_ref.shape[1]
    s = pl.program_id(0) % tiles_per_seq
    xb = x_ref[...].astype(xb_ref.dtype)
    xb_ref[...] = xb

    @pl.when(s == 0)
    def _():
        ext_ref[0:POOL_HALO, :] = jnp.zeros((POOL_HALO, ext_ref.shape[1]), F32)

    @pl.when(s != 0)
    def _():
        ext_ref[0:POOL_HALO, :] = ext_ref[tm:tm + POOL_HALO, :]

    pos1 = lax.broadcasted_iota(jnp.int32, (tm, cg), 0) + (s * tm + 1)
    proj = lambda g: jnp.dot(xb, w_ref[:, g * cg:(g + 1) * cg], preferred_element_type=F32)
    order = sorted(range(len(POOL_WINDOWS)), key=lambda g: -POOL_WINDOWS[g])
    u_next = proj(order[0])
    for k, g in enumerate(order):
        w = POOL_WINDOWS[g]
        cs = slice(g * cg, (g + 1) * cg)
        ug = u_next
        if k + 1 < len(order):
            u_next = proj(order[k + 1])
        ext_ref[POOL_HALO:, cs] = ug
        tot = ug
        for d in range(1, w):
            tot = tot + ext_ref[POOL_HALO - d:POOL_HALO - d + tm, cs]
        count = jnp.minimum(pos1, w).astype(F32)
        pooled = tot / count - ug
        out = jnp.dot(pooled.astype(BF16), wp_ref[g], preferred_element_type=F32) * scale_ref[:, cs]
        pool_ref[:, cs] = out.astype(pool_ref.dtype)


def _proj_heads_kernel(*refs, rope):
    if rope:
        x_ref, w_ref, cos_ref, sin_ref, ci_ref, o_ref, co_ref = refs
    else:
        x_ref, w_ref, ci_ref, o_ref, co_ref = refs
    _cast_slab(ci_ref, co_ref)
    acc = jnp.dot(x_ref[...], w_ref[...], preferred_element_type=F32)
    for hh in range(o_ref.shape[0]):
        t = acc[:, hh * HEAD_DIM:(hh + 1) * HEAD_DIM]
        if rope:
            t = t * cos_ref[...] + pltpu.roll(t, HEAD_DIM // 2, 1) * sin_ref[...]
        o_ref[hh] = t.astype(o_ref.dtype)


def _cast_stream(w_f32, grid, col_tile=None):
    R, C = w_f32.shape
    steps = grid[0] * grid[1]
    rb = R // steps
    assert R % steps == 0 and rb % 16 == 0, (R, steps)
    step = lambda i, j: i * grid[1] + j
    in_spec = pl.BlockSpec((rb, C), lambda i, j: (step(i, j), 0))
    if col_tile is None:
        return in_spec, in_spec, jax.ShapeDtypeStruct((R, C), BF16)
    assert C % col_tile == 0
    out_spec = pl.BlockSpec((C // col_tile, rb, col_tile), lambda i, j: (0, step(i, j), 0))
    return in_spec, out_spec, jax.ShapeDtypeStruct((C // col_tile, R, col_tile), BF16)


def _cast_slab(ci_ref, co_ref):
    if len(co_ref.shape) == 2:
        co_ref[...] = ci_ref[...].astype(co_ref.dtype)
    else:
        ct = co_ref.shape[2]
        for c in range(co_ref.shape[0]):
            co_ref[c] = ci_ref[:, c * ct:(c + 1) * ct].astype(co_ref.dtype)


def _project_pool(x2, w, w_pool, pool_scale, *, seq, col0):
    T, D = x2.shape
    G, cg, _ = w_pool.shape
    P = G * cg
    tm = _pick(seq, 512)
    assert tm >= POOL_HALO >= max(POOL_WINDOWS) - 1 and col0 % P == 0
    once = dict(pipeline_mode=pl.Buffered(1))
    return pl.pallas_call(
        functools.partial(_proj_pool_kernel, tiles_per_seq=seq // tm),
        grid=(T // tm,),
        in_specs=[
            pl.BlockSpec((tm, D), lambda i: (i, 0)),
            pl.BlockSpec((D, P), lambda i: (0, col0 // P), **once),
            pl.BlockSpec((G, cg, cg), lambda i: (0, 0, 0), **once),
            pl.BlockSpec((1, P), lambda i: (0, 0)),
        ],
        out_specs=[pl.BlockSpec((tm, P), lambda i: (i, 0)), pl.BlockSpec((tm, D), lambda i: (i, 0))],
        out_shape=[jax.ShapeDtypeStruct((T, P), BF16), jax.ShapeDtypeStruct((T, D), BF16)],
        scratch_shapes=[pltpu.VMEM((POOL_HALO + tm, P), F32)],
        compiler_params=_params("arbitrary"),
        name="proj_pool",
    )(x2, w, w_pool, pool_scale)


def _project_heads(xb, w, w_cast, *, batch, seq, col0, ncols, cast_col_tile, rope_tables=None):
    T, D = xb.shape
    tm = _pick(seq, 1024)
    tn = _pick(ncols // 2, 1024)
    spt = seq // tm
    hpt = tn // HEAD_DIM
    j0 = col0 // tn
    grid = (T // tm, ncols // tn)
    half = grid[1] // 2
    cast_in, cast_out, cast_shape = _cast_stream(w_cast, grid, cast_col_tile)
    table_spec = pl.BlockSpec((None, tm, HEAD_DIM), lambda i, j: (j // half, i % spt, 0))
    tables = () if rope_tables is None else tuple(rope_tables)
    return pl.pallas_call(
        functools.partial(_proj_heads_kernel, rope=rope_tables is not None),
        grid=grid,
        in_specs=[
            pl.BlockSpec((tm, D), lambda i, j: (i, 0)),
            pl.BlockSpec((D, tn), lambda i, j: (0, j + j0)),
            *([table_spec] * len(tables)),
            cast_in,
        ],
        out_specs=[
            pl.BlockSpec((None, hpt, tm, HEAD_DIM), lambda i, j: (i // spt, j, i % spt, 0)),
            cast_out,
        ],
        out_shape=[jax.ShapeDtypeStruct((batch, ncols // HEAD_DIM, seq, HEAD_DIM), BF16), cast_shape],
        compiler_params=_params("parallel", "arbitrary"),
        name="proj_heads" if rope_tables is None else "proj_rope_heads",
    )(xb, w, *tables, w_cast)


def _moba_kernel(q_ref, k_ref, v_ref, ci_ref, o_ref, co_ref, vtg_ref, kmean_ref, bias_ref):
    blk, grp = MOBA_BLOCK, MOBA_GROUP
    quad = grp * blk
    hps, seq, _ = k_ref.shape
    nb = seq // blk
    ngrp = nb // grp
    nt = (((1,), (1,)), ((), ()))

    _cast_slab(ci_ref, co_ref)

    def setup(h):
        for j in range(nb):
            kj = k_ref[h, j * blk:(j + 1) * blk, :].astype(F32)
            kmean_ref[h, j:j + 1, :] = jnp.sum(kj, axis=0, keepdims=True) * (1.0 / blk)
            vt = v_ref[h, j * blk:(j + 1) * blk, :].astype(F32).T.astype(BF16)
            vtg_ref[h, j // grp, 0:HEAD_DIM, (j % grp) * blk:(j % grp + 1) * blk] = vt
        ones_row = lax.broadcasted_iota(jnp.int32, (vtg_ref.shape[2] - HEAD_DIM, quad), 0) == 0
        for g in range(ngrp):
            vtg_ref[h, g, HEAD_DIM:, :] = jnp.where(ones_row, 1.0, 0.0).astype(BF16)
        gate = lax.dot_general(kmean_ref[h].astype(BF16), q_ref[h], nt, preferred_element_type=F32)
        row = lax.broadcasted_iota(jnp.int32, gate.shape, 0)
        qblk = lax.shift_right_logical(lax.broadcasted_iota(jnp.int32, gate.shape, 1), blk.bit_length() - 1)
        past = row < qblk
        gate = jnp.where(past, gate, -jnp.inf)
        rank = jnp.zeros(gate.shape, jnp.int32)
        for jp in range(nb):
            gj = gate[jp:jp + 1, :]
            beats = (gj > gate) | ((gj == gate) & (jp < row))
            rank = rank + beats.astype(jnp.int32)
        bias = jnp.where(past & (rank < MOBA_TOPK), 0.0, -jnp.inf).astype(F32)
        for g in range(ngrp):
            for qd in range(g, ngrp):
                bias_ref[h, g, qd, 0:grp, :] = bias[g * grp:(g + 1) * grp, qd * quad:(qd + 1) * quad]

    for h in range(hps):
        setup(h)

    kpos = lax.broadcasted_iota(jnp.int32, (blk, blk), 0)
    qpos = lax.broadcasted_iota(jnp.int32, (blk, blk), 1)
    causal = kpos <= qpos

    def scores(k_rows, q):
        return lax.dot_general(k_rows, q, nt, preferred_element_type=F32)

    def tiles_max(tiles):
        m = jnp.max(tiles[0], axis=0, keepdims=True)
        for t in tiles[1:]:
            m = jnp.maximum(m, jnp.max(t, axis=0, keepdims=True))
        return m

    heads = range(hps)

    def diagonal_group(qd, q0):
        s = [[scores(k_ref[h, pl.ds(q0, (r + 1) * blk), :],
                     q_ref[h, pl.ds(pl.multiple_of(q0 + r * blk, blk), blk), :]) for r in range(grp)]
             for h in heads]
        m, p = [], []
        for h in heads:
            mr, pr = [], []
            for r in range(grp):
                brow = bias_ref[h, qd, qd, 0:grp, r * blk:(r + 1) * blk]
                tiles = [s[h][r][t * blk:(t + 1) * blk] + brow[t:t + 1, :] for t in range(r)]
                tiles.append(jnp.where(causal, s[h][r][r * blk:(r + 1) * blk], -jnp.inf))
                mr.append(tiles_max(tiles))
                pr.append(jnp.concatenate([jnp.exp2(t - mr[r]) for t in tiles], axis=0).astype(BF16))
            m.append(jnp.concatenate(mr, axis=1))
            p.append(pr)
        acc = [jnp.concatenate(
            [jnp.dot(vtg_ref[h, qd, :, 0:(r + 1) * blk], p[h][r], preferred_element_type=F32)
             for r in range(grp)], axis=1) for h in heads]
        return tuple((m[h], acc[h]) for h in heads)

    def group_scores(q0, g):
        k0 = pl.multiple_of(g * quad, quad)
        return [scores(k_ref[h, pl.ds(k0, quad), :], q_ref[h, pl.ds(q0, quad), :]) for h in heads]

    def past_group_recentred(qd, q0, g, state):
        s = group_scores(q0, g)
        stats, p = [], []
        for h in heads:
            m = state[h][0]
            brow = bias_ref[h, g, qd, 0:grp, :]
            tiles = [s[h][t * blk:(t + 1) * blk] + brow[t:t + 1, :] for t in range(grp)]
            m_new = jnp.maximum(m, tiles_max(tiles))
            p.append(jnp.concatenate([jnp.exp2(t - m_new) for t in tiles], axis=0).astype(BF16))
            stats.append((m_new, jnp.exp2(m - m_new)))
        pv = [jnp.dot(vtg_ref[h, g], p[h], preferred_element_type=F32) for h in heads]
        return tuple((stats[h][0], stats[h][1] * state[h][1] + pv[h]) for h in heads)

    def past_group(qd, q0, g, state):
        s = group_scores(q0, g)
        p, top = [], []
        for h in heads:
            shift = bias_ref[h, g, qd, 0:grp, :] - state[h][0]
            z = [s[h][t * blk:(t + 1) * blk] + shift[t:t + 1, :] for t in range(grp)]
            top.append(tiles_max(z))
            p.append(jnp.concatenate([jnp.exp2(t) for t in z], axis=0).astype(BF16))
        pv = [jnp.dot(vtg_ref[h, g], p[h], preferred_element_type=F32) for h in heads]
        worst = top[0]
        for h in heads[1:]:
            worst = jnp.maximum(worst, top[h])
        return lax.cond(
            jnp.max(worst) <= SOFTMAX_HEADROOM,
            lambda: tuple((state[h][0], state[h][1] + pv[h]) for h in heads),
            lambda: past_group_recentred(qd, q0, g, state))

    def query_quad(qd, _):
        q0 = pl.multiple_of(qd * quad, quad)
        state = lax.fori_loop(0, qd, functools.partial(past_group, qd, q0), diagonal_group(qd, q0))
        for h, (_, acc) in enumerate(state):
            out = acc[0:HEAD_DIM] / acc[HEAD_DIM:HEAD_DIM + 1]
            o_ref[pl.ds(q0, quad), h * HEAD_DIM:(h + 1) * HEAD_DIM] = out.T.astype(o_ref.dtype)
        return 0

    lax.fori_loop(0, ngrp, query_quad, 0)


def _moba_attention(qk, v, w_cast, *, seq, n_heads):
    batch = qk.shape[0]
    blk, grp = MOBA_BLOCK, MOBA_GROUP
    hps = _pick(n_heads, MOBA_HEADS_PER_STEP)
    nb = seq // blk
    assert seq % (blk * grp) == 0 and blk & (blk - 1) == 0 and grp & (grp - 1) == 0
    heads = lambda off: pl.BlockSpec((None, hps, seq, HEAD_DIM), lambda b, h: (b, h + off, 0, 0))
    grid = (batch, n_heads // hps)
    cast_in, cast_out, cast_shape = _cast_stream(w_cast, grid)
    return pl.pallas_call(
        _moba_kernel,
        grid=grid,
        in_specs=[heads(0), heads(n_heads // hps), heads(0), cast_in],
        out_specs=[pl.BlockSpec((None, seq, hps * HEAD_DIM), lambda b, h: (b, 0, h)), cast_out],
        out_shape=[jax.ShapeDtypeStruct((batch, seq, n_heads * HEAD_DIM), BF16), cast_shape],
        scratch_shapes=[
            pltpu.VMEM((hps, nb // grp, HEAD_DIM + 16, grp * blk), BF16),
            pltpu.VMEM((hps, nb, HEAD_DIM), F32),
            pltpu.VMEM((hps, nb // grp, nb // grp, 8, grp * blk), F32),
        ],
        compiler_params=_params("parallel", "parallel"),
        name="moba_attention",
    )(qk, qk, v, w_cast)


def _layer_norm_tiles(tile, nj, tn, g_ref, b_ref, o_ref, ob_ref=None):
    d = nj * tn
    tot = jnp.sum(tile(0), axis=1, keepdims=True)
    for jj in range(1, nj):
        tot = tot + jnp.sum(tile(jj), axis=1, keepdims=True)
    mu = tot * (1.0 / d)
    sq = jnp.sum(jnp.square(tile(0) - mu), axis=1, keepdims=True)
    for jj in range(1, nj):
        sq = sq + jnp.sum(jnp.square(tile(jj) - mu), axis=1, keepdims=True)
    rstd = lax.rsqrt(sq * (1.0 / d) + LN_EPS)
    for jj in range(nj):
        cs = slice(jj * tn, (jj + 1) * tn)
        y = (tile(jj) - mu) * rstd * g_ref[:, cs] + b_ref[:, cs]
        o_ref[:, cs] = y
        if ob_ref is not None:
            ob_ref[:, cs] = y.astype(ob_ref.dtype)


def _outproj_ln_kernel(a_ref, p_ref, w_ref, x_ref, g_ref, b_ref, o_ref, ob_ref, *, alpha, tn):
    ka = a_ref.shape[1]
    nj = o_ref.shape[1] // tn
    for jj in range(nj):
        cs = slice(jj * tn, (jj + 1) * tn)
        mix = jnp.dot(a_ref[...], w_ref[0:ka, cs], preferred_element_type=F32)
        mix = mix + jnp.dot(p_ref[...], w_ref[ka:, cs], preferred_element_type=F32)
        o_ref[:, cs] = alpha * x_ref[:, cs] + mix
    _layer_norm_tiles(lambda jj: o_ref[:, jj * tn:(jj + 1) * tn], nj, tn, g_ref, b_ref, o_ref, ob_ref)


def _outproj_ln(attn, pool, w_out, x2, g, b, *, alpha):
    T, D = x2.shape
    ka, kp = attn.shape[1], pool.shape[1]
    tm = _pick(T, 256)
    return pl.pallas_call(
        functools.partial(_outproj_ln_kernel, alpha=alpha, tn=_pick(D, OUTPROJ_COL_TILE)),
        grid=(T // tm,),
        in_specs=[
            pl.BlockSpec((tm, ka), lambda i: (i, 0)),
            pl.BlockSpec((tm, kp), lambda i: (i, 0)),
            pl.BlockSpec((ka + kp, D), lambda i: (0, 0), pipeline_mode=pl.Buffered(1)),
            pl.BlockSpec((tm, D), lambda i: (i, 0)),
            pl.BlockSpec((1, D), lambda i: (0, 0)),
            pl.BlockSpec((1, D), lambda i: (0, 0)),
        ],
        out_specs=[pl.BlockSpec((tm, D), lambda i: (i, 0))] * 2,
        out_shape=[jax.ShapeDtypeStruct((T, D), F32), jax.ShapeDtypeStruct((T, D), BF16)],
        compiler_params=_params("parallel"),
        name="outproj_ln1",
    )(attn, pool, w_out, x2, g, b)


def _ffn_ln_kernel(hb_ref, h_hbm, w1_ref, w2_ref, g_ref, b_ref, o_ref, res_ref, res_sem, *, alpha, tn):
    f = pl.program_id(1)
    tm = o_ref.shape[0]
    nj = o_ref.shape[1] // tn
    rows = pl.ds(pl.multiple_of(pl.program_id(0) * tm, tm), tm)
    residual_copy = pltpu.make_async_copy(h_hbm.at[rows, :], res_ref, res_sem)

    @pl.when(f == 0)
    def _():
        residual_copy.start()
        o_ref[...] = jnp.zeros_like(o_ref)

    hid = jnp.dot(hb_ref[...], w1_ref[...], preferred_element_type=F32)
    hid = jnp.square(jnp.maximum(hid, 0.0)).astype(BF16)
    for jj in range(nj):
        cs = slice(jj * tn, (jj + 1) * tn)
        o_ref[:, cs] += jnp.dot(hid, w2_ref[:, cs], preferred_element_type=F32)

    @pl.when(f == pl.num_programs(1) - 1)
    def _():
        residual_copy.wait()
        o_ref[...] = alpha * res_ref[...] + o_ref[...]
        _layer_norm_tiles(lambda jj: o_ref[:, jj * tn:(jj + 1) * tn], nj, tn, g_ref, b_ref, o_ref)


def _ffn_ln(hb, h, w1_t, w2, g, b, *, alpha):
    T, D = h.shape
    nf, _, tf = w1_t.shape
    tm = _pick(T, 512)
    tn = _pick(D, 512)
    return pl.pallas_call(
        functools.partial(_ffn_ln_kernel, alpha=alpha, tn=tn),
        grid=(T // tm, nf),
        in_specs=[
            pl.BlockSpec((tm, D), lambda i, f: (i, 0)),
            pl.BlockSpec(memory_space=pl.ANY),
            pl.BlockSpec((None, D, tf), lambda i, f: (f, 0, 0)),
            pl.BlockSpec((tf, D), lambda i, f: (f, 0)),
            pl.BlockSpec((1, D), lambda i, f: (0, 0)),
            pl.BlockSpec((1, D), lambda i, f: (0, 0)),
        ],
        out_specs=pl.BlockSpec((tm, D), lambda i, f: (i, 0)),
        out_shape=jax.ShapeDtypeStruct((T, D), F32),
        scratch_shapes=[pltpu.VMEM((tm, D), F32), pltpu.SemaphoreType.DMA(())],
        compiler_params=_params("parallel", "arbitrary"),
        name="ffn_ln2",
    )(hb, h, w1_t, w2, g, b)


def _rope_tables(seq, q_scale):
    inv_freq = 1.0 / (ROPE_THETA ** (jnp.arange(0, HEAD_DIM, 2, dtype=F32) / HEAD_DIM))
    ang = jnp.arange(seq, dtype=F32)[:, None] * inv_freq[None, :]
    cos, sin = jnp.cos(ang), jnp.sin(ang)
    cos = jnp.concatenate([cos, cos], axis=-1)
    sin = jnp.concatenate([-sin, sin], axis=-1)
    return jnp.stack([q_scale * cos, cos]), jnp.stack([q_scale * sin, sin])


def kernel(x, w_in, w_pool, pool_scale, w_out, ln1_g, ln1_b, w_ff1, w_ff2, ln2_g, ln2_b):
    B, S, D = x.shape
    depth = w_in.shape[0]
    attn_w = D // 2
    n_heads = attn_w // HEAD_DIM
    pool_w = D - attn_w
    alpha = (2.0 * depth) ** 0.25
    rope_tables = _rope_tables(S, HEAD_DIM ** -0.5 * LOG2_E)

    x2 = x.reshape(B * S, D)
    for l in range(depth):
        w_in_b = w_in[l].astype(BF16)
        pool, xb = _project_pool(x2, w_in_b, w_pool[l].astype(BF16), pool_scale[l].reshape(1, pool_w),
                                 seq=S, col0=3 * attn_w)
        qk, w_ff1_t = _project_heads(xb, w_in_b, w_ff1[l], batch=B, seq=S, col0=0, ncols=2 * attn_w,
                                     cast_col_tile=_pick(w_ff1.shape[2], FFN_HIDDEN_TILE),
                                     rope_tables=rope_tables)
        v, w_out_b = _project_heads(xb, w_in_b, w_out[l], batch=B, seq=S, col0=2 * attn_w, ncols=attn_w,
                                    cast_col_tile=None)

        attn, w_ff2_b = _moba_attention(qk, v, w_ff2[l], seq=S, n_heads=n_heads)
        h, hb = _outproj_ln(attn.reshape(B * S, attn_w), pool, w_out_b, x2,
                            ln1_g[l].reshape(1, D), ln1_b[l].reshape(1, D), alpha=alpha)
        x2 = _ffn_ln(hb, h, w_ff1_t, w_ff2_b, ln2_g[l].reshape(1, D), ln2_b[l].reshape(1, D), alpha=alpha)
    return x2.reshape(B, S, D)
```

```python
import functools

import jax
import jax.numpy as jnp
from jax import lax
from jax.experimental import pallas as pl
from jax.experimental.pallas import tpu as pltpu

HEAD_DIM = 128
POOL_WINDOWS = (2, 4, 8, 16)
MOBA_BLOCK = 256
MOBA_TOPK = 3
MOBA_GROUP = 4
MOBA_HEADS_PER_STEP = 2
SOFTMAX_HEADROOM = 64.0
LOG2_E = 1.4426950408889634
FFN_HIDDEN_TILE = 512
OUTPROJ_COL_TILE = 1024
ROPE_THETA = 10000.0
LN_EPS = 1e-5
POOL_HALO = 16

V7X_SUBLANES_F32 = 8
V7X_SUBLANES_BF16 = 16
V7X_VMEM_LIMIT_BYTES = 62 * 1024 * 1024

F32 = jnp.float32
BF16 = jnp.bfloat16


def _pick(n, pref):
    t = min(pref, n)
    while n % t:
        t //= 2
    return t


def _params(*sem):
    return pltpu.CompilerParams(dimension_semantics=sem, vmem_limit_bytes=V7X_VMEM_LIMIT_BYTES)


def _proj_pool_kernel(x_ref, w_ref, wp_ref, scale_ref, pool_ref, xb_ref, ext_ref, *, tiles_per_seq):
    tm = x_ref.shape[0]
    cg = wp_ref.shape[1]
    s = pl.program_id(0) % tiles_per_seq
    xb = x_ref[...].astype(xb_ref.dtype)
    xb_ref[...] = xb

    @pl.when(s == 0)
    def _():
        ext_ref[0:POOL_HALO, :] = jnp.zeros((POOL_HALO, ext_ref.shape[1]), F32)

    @pl.when(s != 0)
    def _():
        ext_ref[0:POOL_HALO, :] = ext_ref[tm:tm + POOL_HALO, :]

    pos1 = lax.broadcasted_iota(jnp.int32, (tm, cg), 0) + (s * tm + 1)
    proj = lambda g: jnp.dot(xb, w_ref[:, g * cg:(g + 1) * cg], preferred_element_type=F32)
    order = sorted(range(len(POOL_WINDOWS)), key=lambda g: -POOL_WINDOWS[g])
    u_next = proj(order[0])
    for k, g in enumerate(order):
        w = POOL_WINDOWS[g]
        cs = slice(g * cg, (g + 1) * cg)
        ug = u_next
        if k + 1 < len(order):
            u_next = proj(order[k + 1])
        ext_ref[POOL_HALO:, cs] = ug
        tot = ug
        for d in range(1, w):
            tot = tot + ext_ref[POOL_HALO - d:POOL_HALO - d + tm, cs]
        count = jnp.minimum(pos1, w).astype(F32)
        pooled = tot / count - ug
        out = jnp.dot(pooled.astype(BF16), wp_ref[g], preferred_element_type=F32) * scale_ref[:, cs]
        pool_ref[:, cs] = out.astype(pool_ref.dtype)


def _proj_heads_kernel(*refs, rope):
    if rope:
        x_ref, w_ref, rope_ref, ci_ref, o_ref, co_ref = refs
    else:
        x_ref, w_ref, ci_ref, o_ref, co_ref = refs
    co_ref[...] = ci_ref[...].astype(co_ref.dtype)
    acc = jnp.dot(x_ref[...], w_ref[...], preferred_element_type=F32)
    for hh in range(o_ref.shape[0]):
        t = acc[:, hh * HEAD_DIM:(hh + 1) * HEAD_DIM]
        if rope:
            t = t * rope_ref[0] + pltpu.roll(t, HEAD_DIM // 2, 1) * rope_ref[1]
        o_ref[hh] = t.astype(o_ref.dtype)


def _cast_stream(w_f32, grid):
    R, C = w_f32.shape
    steps = grid[0] * grid[1]
    rb = R // steps
    assert R % steps == 0 and rb % V7X_SUBLANES_BF16 == 0, (R, steps)
    spec = pl.BlockSpec((rb, C), lambda i, j: (i * grid[1] + j, 0))
    return spec, jax.ShapeDtypeStruct((R, C), BF16)


def _project_pool(x2, w, w_pool, pool_scale, *, seq, col0):
    T, D = x2.shape
    G, cg, _ = w_pool.shape
    P = G * cg
    tm = _pick(seq, 512)
    assert tm >= POOL_HALO >= max(POOL_WINDOWS) - 1 and col0 % P == 0
    return pl.pallas_call(
        functools.partial(_proj_pool_kernel, tiles_per_seq=seq // tm),
        grid=(T // tm,),
        in_specs=[
            pl.BlockSpec((tm, D), lambda i: (i, 0)),
            pl.BlockSpec((D, P), lambda i: (0, col0 // P), pipeline_mode=pl.Buffered(1)),
            pl.BlockSpec((G, cg, cg), lambda i: (0, 0, 0)),
            pl.BlockSpec((1, P), lambda i: (0, 0)),
        ],
        out_specs=[pl.BlockSpec((tm, P), lambda i: (i, 0)), pl.BlockSpec((tm, D), lambda i: (i, 0))],
        out_shape=[jax.ShapeDtypeStruct((T, P), BF16), jax.ShapeDtypeStruct((T, D), BF16)],
        scratch_shapes=[pltpu.VMEM((POOL_HALO + tm, P), F32)],
        compiler_params=_params("arbitrary"),
        name="proj_pool",
    )(x2, w, w_pool, pool_scale)


def _project_heads(xb, w, w_cast, *, batch, seq, col0, ncols, rope_table=None):
    T, D = xb.shape
    tm = _pick(seq, 1024)
    tn = _pick(ncols // 2, 1024)
    spt = seq // tm
    hpt = tn // HEAD_DIM
    j0 = col0 // tn
    grid = (T // tm, ncols // tn)
    half = grid[1] // 2
    cast_spec, cast_shape = _cast_stream(w_cast, grid)
    table_spec = pl.BlockSpec((None, 2, tm, HEAD_DIM), lambda i, j: (j // half, 0, i % spt, 0))
    tables = () if rope_table is None else (rope_table,)
    return pl.pallas_call(
        functools.partial(_proj_heads_kernel, rope=rope_table is not None),
        grid=grid,
        in_specs=[
            pl.BlockSpec((tm, D), lambda i, j: (i, 0)),
            pl.BlockSpec((D, tn), lambda i, j: (0, j + j0)),
            *([table_spec] * len(tables)),
            cast_spec,
        ],
        out_specs=[
            pl.BlockSpec((None, hpt, tm, HEAD_DIM), lambda i, j: (i // spt, j, i % spt, 0)),
            cast_spec,
        ],
        out_shape=[jax.ShapeDtypeStruct((batch, ncols // HEAD_DIM, seq, HEAD_DIM), BF16), cast_shape],
        compiler_params=_params("parallel", "arbitrary"),
        name="proj_heads" if rope_table is None else "proj_rope_heads",
    )(xb, w, *tables, w_cast)


def _moba_kernel(q_ref, k_ref, v_ref, ci_ref, o_ref, co_ref, vtg_ref, kmean_ref, bias_ref):
    blk, grp = MOBA_BLOCK, MOBA_GROUP
    quad = grp * blk
    hps, seq, _ = k_ref.shape
    nb = seq // blk
    ngrp = nb // grp
    nt = (((1,), (1,)), ((), ()))

    co_ref[...] = ci_ref[...].astype(co_ref.dtype)

    def setup(h):
        for j in range(nb):
            kj = k_ref[h, j * blk:(j + 1) * blk, :].astype(F32)
            kmean_ref[h, j:j + 1, :] = jnp.sum(kj, axis=0, keepdims=True) * (1.0 / blk)
            vt = v_ref[h, j * blk:(j + 1) * blk, :].astype(F32).T.astype(BF16)
            vtg_ref[h, j // grp, 0:HEAD_DIM, (j % grp) * blk:(j % grp + 1) * blk] = vt
        ones_row = lax.broadcasted_iota(jnp.int32, (vtg_ref.shape[2] - HEAD_DIM, quad), 0) == 0
        for g in range(ngrp):
            vtg_ref[h, g, HEAD_DIM:, :] = jnp.where(ones_row, 1.0, 0.0).astype(BF16)
        gate = lax.dot_general(kmean_ref[h].astype(BF16), q_ref[h], nt, preferred_element_type=F32)
        row = lax.broadcasted_iota(jnp.int32, gate.shape, 0)
        qblk = lax.shift_right_logical(lax.broadcasted_iota(jnp.int32, gate.shape, 1), blk.bit_length() - 1)
        past = row < qblk
        gate = jnp.where(past, gate, -jnp.inf)
        rank = jnp.zeros(gate.shape, jnp.int32)
        for jp in range(nb):
            gj = gate[jp:jp + 1, :]
            beats = (gj > gate) | ((gj == gate) & (jp < row))
            rank = rank + beats.astype(jnp.int32)
        bias = jnp.where(past & (rank < MOBA_TOPK), 0.0, -jnp.inf).astype(F32)
        for g in range(ngrp):
            for qd in range(g, ngrp):
                bias_ref[h, g, qd, 0:grp, :] = bias[g * grp:(g + 1) * grp, qd * quad:(qd + 1) * quad]

    for h in range(hps):
        setup(h)

    kpos = lax.broadcasted_iota(jnp.int32, (blk, blk), 0)
    qpos = lax.broadcasted_iota(jnp.int32, (blk, blk), 1)
    causal = kpos <= qpos

    def scores(k_rows, q):
        return lax.dot_general(k_rows, q, nt, preferred_element_type=F32)

    def tiles_max(tiles):
        m = jnp.max(tiles[0], axis=0, keepdims=True)
        for t in tiles[1:]:
            m = jnp.maximum(m, jnp.max(t, axis=0, keepdims=True))
        return m

    heads = range(hps)

    def diagonal_group(qd, q0):
        s = [[scores(k_ref[h, pl.ds(q0, (r + 1) * blk), :],
                     q_ref[h, pl.ds(pl.multiple_of(q0 + r * blk, blk), blk), :]) for r in range(grp)]
             for h in heads]
        m, p = [], []
        for h in heads:
            mr, pr = [], []
            for r in range(grp):
                brow = bias_ref[h, qd, qd, 0:grp, r * blk:(r + 1) * blk]
                tiles = [s[h][r][t * blk:(t + 1) * blk] + brow[t:t + 1, :] for t in range(r)]
                tiles.append(jnp.where(causal, s[h][r][r * blk:(r + 1) * blk], -jnp.inf))
                mr.append(tiles_max(tiles))
                pr.append(jnp.concatenate([jnp.exp2(t - mr[r]) for t in tiles], axis=0).astype(BF16))
            m.append(jnp.concatenate(mr, axis=1))
            p.append(pr)
        acc = [jnp.concatenate(
            [jnp.dot(vtg_ref[h, qd, :, 0:(r + 1) * blk], p[h][r], preferred_element_type=F32)
             for r in range(grp)], axis=1) for h in heads]
        return tuple((m[h], acc[h]) for h in heads)

    def group_scores(q0, g):
        k0 = pl.multiple_of(g * quad, quad)
        return [scores(k_ref[h, pl.ds(k0, quad), :], q_ref[h, pl.ds(q0, quad), :]) for h in heads]

    def past_group_recentred(qd, q0, g, state):
        s = group_scores(q0, g)
        stats, p = [], []
        for h in heads:
            m = state[h][0]
            brow = bias_ref[h, g, qd, 0:grp, :]
            tiles = [s[h][t * blk:(t + 1) * blk] + brow[t:t + 1, :] for t in range(grp)]
            m_new = jnp.maximum(m, tiles_max(tiles))
            p.append(jnp.concatenate([jnp.exp2(t - m_new) for t in tiles], axis=0).astype(BF16))
            stats.append((m_new, jnp.exp2(m - m_new)))
        pv = [jnp.dot(vtg_ref[h, g], p[h], preferred_element_type=F32) for h in heads]
        return tuple((stats[h][0], stats[h][1] * state[h][1] + pv[h]) for h in heads)

    def past_group(qd, q0, g, state):
        s = group_scores(q0, g)
        p, top = [], []
        for h in heads:
            shift = bias_ref[h, g, qd, 0:grp, :] - state[h][0]
            z = [s[h][t * blk:(t + 1) * blk] + shift[t:t + 1, :] for t in range(grp)]
            top.append(tiles_max(z))
            p.append(jnp.concatenate([jnp.exp2(t) for t in z], axis=0).astype(BF16))
        pv = [jnp.dot(vtg_ref[h, g], p[h], preferred_element_type=F32) for h in heads]
        worst = top[0]
        for h in heads[1:]:
            worst = jnp.maximum(worst, top[h])
        return lax.cond(
            jnp.max(worst) <= SOFTMAX_HEADROOM,
            lambda: tuple((state[h][0], state[h][1] + pv[h]) for h in heads),
            lambda: past_group_recentred(qd, q0, g, state))

    def query_quad(qd, _):
        q0 = pl.multiple_of(qd * quad, quad)
        state = lax.fori_loop(0, qd, functools.partial(past_group, qd, q0), diagonal_group(qd, q0))
        for h, (_, acc) in enumerate(state):
            out = acc[0:HEAD_DIM] / acc[HEAD_DIM:HEAD_DIM + 1]
            o_ref[pl.ds(q0, quad), h * HEAD_DIM:(h + 1) * HEAD_DIM] = out.T.astype(o_ref.dtype)
        return 0

    lax.fori_loop(0, ngrp, query_quad, 0)


def _moba_attention(qk, v, w_cast, *, seq, n_heads):
    batch = qk.shape[0]
    blk, grp = MOBA_BLOCK, MOBA_GROUP
    hps = _pick(n_heads, MOBA_HEADS_PER_STEP)
    nb = seq // blk
    assert seq % (blk * grp) == 0 and blk & (blk - 1) == 0 and grp <= V7X_SUBLANES_F32
    heads = lambda off: pl.BlockSpec((None, hps, seq, HEAD_DIM), lambda b, h: (b, h + off, 0, 0))
    grid = (batch, n_heads // hps)
    cast_spec, cast_shape = _cast_stream(w_cast, grid)
    return pl.pallas_call(
        _moba_kernel,
        grid=grid,
        in_specs=[heads(0), heads(n_heads // hps), heads(0), cast_spec],
        out_specs=[pl.BlockSpec((None, seq, hps * HEAD_DIM), lambda b, h: (b, 0, h)), cast_spec],
        out_shape=[jax.ShapeDtypeStruct((batch, seq, n_heads * HEAD_DIM), BF16), cast_shape],
        scratch_shapes=[
            pltpu.VMEM((hps, nb // grp, HEAD_DIM + V7X_SUBLANES_BF16, grp * blk), BF16),
            pltpu.VMEM((hps, nb, HEAD_DIM), F32),
            pltpu.VMEM((hps, nb // grp, nb // grp, V7X_SUBLANES_F32, grp * blk), F32),
        ],
        compiler_params=_params("parallel", "parallel"),
        name="moba_attention",
    )(qk, qk, v, w_cast)


def _layer_norm_tiles(tile, nj, tn, gb_ref, o_ref, ob_ref=None):
    d = nj * tn
    tot = jnp.sum(tile(0), axis=1, keepdims=True)
    for jj in range(1, nj):
        tot = tot + jnp.sum(tile(jj), axis=1, keepdims=True)
    mu = tot * (1.0 / d)
    sq = jnp.sum(jnp.square(tile(0) - mu), axis=1, keepdims=True)
    for jj in range(1, nj):
        sq = sq + jnp.sum(jnp.square(tile(jj) - mu), axis=1, keepdims=True)
    rstd = lax.rsqrt(sq * (1.0 / d) + LN_EPS)
    for jj in range(nj):
        cs = slice(jj * tn, (jj + 1) * tn)
        y = (tile(jj) - mu) * rstd * gb_ref[0:1, cs] + gb_ref[1:2, cs]
        o_ref[:, cs] = y
        if ob_ref is not None:
            ob_ref[:, cs] = y.astype(ob_ref.dtype)


def _outproj_ln_kernel(a_ref, p_ref, w_ref, x_ref, gb_ref, o_ref, ob_ref, *, alpha, tn):
    ka = a_ref.shape[1]
    nj = o_ref.shape[1] // tn
    for jj in range(nj):
        cs = slice(jj * tn, (jj + 1) * tn)
        mix = jnp.dot(a_ref[...], w_ref[0:ka, cs], preferred_element_type=F32)
        mix = mix + jnp.dot(p_ref[...], w_ref[ka:, cs], preferred_element_type=F32)
        o_ref[:, cs] = alpha * x_ref[:, cs] + mix
    _layer_norm_tiles(lambda jj: o_ref[:, jj * tn:(jj + 1) * tn], nj, tn, gb_ref, o_ref, ob_ref)


def _outproj_ln(attn, pool, w_out, x2, gb, *, alpha):
    T, D = x2.shape
    ka, kp = attn.shape[1], pool.shape[1]
    tm = _pick(T, 256)
    return pl.pallas_call(
        functools.partial(_outproj_ln_kernel, alpha=alpha, tn=_pick(D, OUTPROJ_COL_TILE)),
        grid=(T // tm,),
        in_specs=[
            pl.BlockSpec((tm, ka), lambda i: (i, 0)),
            pl.BlockSpec((tm, kp), lambda i: (i, 0)),
            pl.BlockSpec((ka + kp, D), lambda i: (0, 0), pipeline_mode=pl.Buffered(1)),
            pl.BlockSpec((tm, D), lambda i: (i, 0)),
            pl.BlockSpec((2, D), lambda i: (0, 0)),
        ],
        out_specs=[pl.BlockSpec((tm, D), lambda i: (i, 0))] * 2,
        out_shape=[jax.ShapeDtypeStruct((T, D), F32), jax.ShapeDtypeStruct((T, D), BF16)],
        compiler_params=_params("parallel"),
        name="outproj_ln1",
    )(attn, pool, w_out, x2, gb)


def _ffn_ln_kernel(hb_ref, h_hbm, w1_ref, w2_ref, gb_ref, o_ref, res_ref, res_sem, *, alpha, tn):
    f = pl.program_id(1)
    tm = o_ref.shape[0]
    nj = o_ref.shape[1] // tn
    rows = pl.ds(pl.multiple_of(pl.program_id(0) * tm, tm), tm)
    residual_copy = pltpu.make_async_copy(h_hbm.at[rows, :], res_ref, res_sem)

    @pl.when(f == 0)
    def _():
        residual_copy.start()
        o_ref[...] = jnp.zeros_like(o_ref)

    hid = jnp.dot(hb_ref[...], w1_ref[...], preferred_element_type=F32)
    hid = jnp.square(jnp.maximum(hid, 0.0)).astype(BF16)
    for jj in range(nj):
        cs = slice(jj * tn, (jj + 1) * tn)
        o_ref[:, cs] += jnp.dot(hid, w2_ref[:, cs], preferred_element_type=F32)

    @pl.when(f == pl.num_programs(1) - 1)
    def _():
        residual_copy.wait()
        o_ref[...] = alpha * res_ref[...] + o_ref[...]
        _layer_norm_tiles(lambda jj: o_ref[:, jj * tn:(jj + 1) * tn], nj, tn, gb_ref, o_ref)


def _ffn_ln(hb, h, w1, w2, gb, *, alpha):
    T, D = h.shape
    F = w1.shape[1]
    tm = _pick(T, 512)
    tf = _pick(F, FFN_HIDDEN_TILE)
    tn = _pick(D, 512)
    return pl.pallas_call(
        functools.partial(_ffn_ln_kernel, alpha=alpha, tn=tn),
        grid=(T // tm, F // tf),
        in_specs=[
            pl.BlockSpec((tm, D), lambda i, f: (i, 0)),
            pl.BlockSpec(memory_space=pl.ANY),
            pl.BlockSpec((D, tf), lambda i, f: (0, f)),
            pl.BlockSpec((tf, D), lambda i, f: (f, 0)),
            pl.BlockSpec((2, D), lambda i, f: (0, 0)),
        ],
        out_specs=pl.BlockSpec((tm, D), lambda i, f: (i, 0)),
        out_shape=jax.ShapeDtypeStruct((T, D), F32),
        scratch_shapes=[pltpu.VMEM((tm, D), F32), pltpu.SemaphoreType.DMA(())],
        compiler_params=_params("parallel", "arbitrary"),
        name="ffn_ln2",
    )(hb, h, w1, w2, gb)


def _rope_table(seq, q_scale):
    inv_freq = 1.0 / (ROPE_THETA ** (jnp.arange(0, HEAD_DIM, 2, dtype=F32) / HEAD_DIM))
    ang = jnp.arange(seq, dtype=F32)[:, None] * inv_freq[None, :]
    cos, sin = jnp.cos(ang), jnp.sin(ang)
    cos = jnp.concatenate([cos, cos], axis=-1)
    sin = jnp.concatenate([-sin, sin], axis=-1)
    return jnp.stack([jnp.stack([q_scale * cos, q_scale * sin]), jnp.stack([cos, sin])])


def kernel(x, w_in, w_pool, pool_scale, w_out, ln1_g, ln1_b, w_ff1, w_ff2, ln2_g, ln2_b):
    B, S, D = x.shape
    depth = w_in.shape[0]
    attn_w = D // 2
    n_heads = attn_w // HEAD_DIM
    pool_w = D - attn_w
    alpha = (2.0 * depth) ** 0.25
    rope_table = _rope_table(S, HEAD_DIM ** -0.5 * LOG2_E)

    x2 = x.reshape(B * S, D)
    for l in range(depth):
        w_in_b = w_in[l].astype(BF16)
        pool, xb = _project_pool(x2, w_in_b, w_pool[l].astype(BF16), pool_scale[l].reshape(1, pool_w),
                                 seq=S, col0=3 * attn_w)
        qk, w_ff1_b = _project_heads(xb, w_in_b, w_ff1[l], batch=B, seq=S, col0=0, ncols=2 * attn_w,
                                     rope_table=rope_table)
        v, w_out_b = _project_heads(xb, w_in_b, w_out[l], batch=B, seq=S, col0=2 * attn_w, ncols=attn_w)

        attn, w_ff2_b = _moba_attention(qk, v, w_ff2[l], seq=S, n_heads=n_heads)
        h, hb = _outproj_ln(attn.reshape(B * S, attn_w), pool, w_out_b, x2,
                            jnp.stack([ln1_g[l], ln1_b[l]]), alpha=alpha)
        x2 = _ffn_ln(hb, h, w_ff1_b, w_ff2_b, jnp.stack([ln2_g[l], ln2_b[l]]), alpha=alpha)
    return x2.reshape(B, S, D)
```

```python
import functools

import jax
import jax.numpy as jnp
from jax import lax
from jax.experimental import pallas as pl
from jax.experimental.pallas import tpu as pltpu

HEAD_DIM = 128
POOL_WINDOWS = (2, 4, 8, 16)
MOBA_BLOCK = 256
MOBA_TOPK = 3
MOBA_GROUP = 4
MOBA_HEADS_PER_STEP = 2
SOFTMAX_HEADROOM = 64.0
LOG2_E = 1.4426950408889634
FFN_HIDDEN_TILE = 512
OUTPROJ_COL_TILE = 1024
ROPE_THETA = 10000.0
LN_EPS = 1e-5
POOL_HALO = 16

V7X_SUBLANES_F32 = 8
V7X_SUBLANES_BF16 = 16
V7X_VMEM_LIMIT_BYTES = 62 * 1024 * 1024

F32 = jnp.float32
BF16 = jnp.bfloat16


def _pick(n, pref):
    t = min(pref, n)
    while n % t:
        t //= 2
    return t


def _params(*sem):
    return pltpu.CompilerParams(dimension_semantics=sem, vmem_limit_bytes=V7X_VMEM_LIMIT_BYTES)


def _proj_pool_kernel(x_ref, w_ref, wp_ref, scale_ref, pool_ref, xb_ref, ext_ref, *, tiles_per_seq):
    tm = x_ref.shape[0]
    cg = wp_ref.shape[1]
    s = pl.program_id(0) % tiles_per_seq
    xb = x_ref[...].astype(xb_ref.dtype)
    xb_ref[...] = xb

    @pl.when(s == 0)
    def _():
        ext_ref[0:POOL_HALO, :] = jnp.zeros((POOL_HALO, ext_ref.shape[1]), F32)

    @pl.when(s != 0)
    def _():
        ext_ref[0:POOL_HALO, :] = ext_ref[tm:tm + POOL_HALO, :]

    pos1 = lax.broadcasted_iota(jnp.int32, (tm, cg), 0) + (s * tm + 1)
    proj = lambda g: jnp.dot(xb, w_ref[:, g * cg:(g + 1) * cg], preferred_element_type=F32)
    order = sorted(range(len(POOL_WINDOWS)), key=lambda g: -POOL_WINDOWS[g])
    u_next = proj(order[0])
    for k, g in enumerate(order):
        w = POOL_WINDOWS[g]
        cs = slice(g * cg, (g + 1) * cg)
        ug = u_next
        if k + 1 < len(order):
            u_next = proj(order[k + 1])
        ext_ref[POOL_HALO:, cs] = ug
        tot = ug
        for d in range(1, w):
            tot = tot + ext_ref[POOL_HALO - d:POOL_HALO - d + tm, cs]
        count = jnp.minimum(pos1, w).astype(F32)
        pooled = tot / count - ug
        out = jnp.dot(pooled.astype(BF16), wp_ref[g], preferred_element_type=F32) * scale_ref[:, cs]
        pool_ref[:, cs] = out.astype(pool_ref.dtype)


def _proj_heads_kernel(*refs, rope, n_cast):
    x_ref, w_ref = refs[:2]
    rope_ref = refs[2] if rope else None
    ci_refs = refs[len(refs) - 2 * n_cast - 1:len(refs) - n_cast - 1]
    o_ref = refs[len(refs) - n_cast - 1]
    co_refs = refs[len(refs) - n_cast:]
    for ci_ref, co_ref in zip(ci_refs, co_refs):
        co_ref[...] = ci_ref[...].astype(co_ref.dtype)
    acc = jnp.dot(x_ref[...], w_ref[...], preferred_element_type=F32)
    for hh in range(o_ref.shape[0]):
        t = acc[:, hh * HEAD_DIM:(hh + 1) * HEAD_DIM]
        if rope:
            t = t * rope_ref[0] + pltpu.roll(t, HEAD_DIM // 2, 1) * rope_ref[1]
        o_ref[hh] = t.astype(o_ref.dtype)


def _cast_stream(w_f32, grid, ncols=None):
    R = w_f32.shape[0]
    C = w_f32.shape[1] if ncols is None else ncols
    assert C % HEAD_DIM == 0
    steps = grid[0] * grid[1]
    rb = R // steps
    assert R % steps == 0 and rb % V7X_SUBLANES_BF16 == 0, (R, steps)
    spec = pl.BlockSpec((rb, C), lambda i, j: (i * grid[1] + j, 0))
    return spec, jax.ShapeDtypeStruct((R, C), BF16)


def _project_pool(x2, w, w_pool, pool_scale, *, seq, col0):
    T, D = x2.shape
    G, cg, _ = w_pool.shape
    P = G * cg
    tm = _pick(seq, 512)
    assert tm >= POOL_HALO >= max(POOL_WINDOWS) - 1 and col0 % P == 0
    return pl.pallas_call(
        functools.partial(_proj_pool_kernel, tiles_per_seq=seq // tm),
        grid=(T // tm,),
        in_specs=[
            pl.BlockSpec((tm, D), lambda i: (i, 0)),
            pl.BlockSpec((D, P), lambda i: (0, col0 // P), pipeline_mode=pl.Buffered(1)),
            pl.BlockSpec((G, cg, cg), lambda i: (0, 0, 0)),
            pl.BlockSpec((1, P), lambda i: (0, 0)),
        ],
        out_specs=[pl.BlockSpec((tm, P), lambda i: (i, 0)), pl.BlockSpec((tm, D), lambda i: (i, 0))],
        out_shape=[jax.ShapeDtypeStruct((T, P), BF16), jax.ShapeDtypeStruct((T, D), BF16)],
        scratch_shapes=[pltpu.VMEM((POOL_HALO + tm, P), F32)],
        compiler_params=_params("arbitrary"),
        name="proj_pool",
    )(x2, w, w_pool, pool_scale)


def _project_heads(xb, w, w_casts, *, batch, seq, col0, ncols, rope_table=None):
    T, D = xb.shape
    tm = _pick(seq, 1024)
    tn = _pick(ncols // 2, 1024)
    spt = seq // tm
    hpt = tn // HEAD_DIM
    j0 = col0 // tn
    grid = (T // tm, ncols // tn)
    half = grid[1] // 2
    cast_specs, cast_shapes = zip(*(_cast_stream(a, grid, n) for a, n in w_casts))
    table_spec = pl.BlockSpec((None, 2, tm, HEAD_DIM), lambda i, j: (j // half, 0, i % spt, 0))
    tables = () if rope_table is None else (rope_table,)
    return pl.pallas_call(
        functools.partial(_proj_heads_kernel, rope=rope_table is not None, n_cast=len(w_casts)),
        grid=grid,
        in_specs=[
            pl.BlockSpec((tm, D), lambda i, j: (i, 0)),
            pl.BlockSpec((D, tn), lambda i, j: (0, j + j0)),
            *([table_spec] * len(tables)),
            *cast_specs,
        ],
        out_specs=[
            pl.BlockSpec((None, hpt, tm, HEAD_DIM), lambda i, j: (i // spt, j, i % spt, 0)),
            *cast_specs,
        ],
        out_shape=[jax.ShapeDtypeStruct((batch, ncols // HEAD_DIM, seq, HEAD_DIM), BF16), *cast_shapes],
        compiler_params=_params("parallel", "arbitrary"),
        name="proj_heads" if rope_table is None else "proj_rope_heads",
    )(xb, w, *tables, *(a for a, _ in w_casts))


def _moba_kernel(q_ref, k_ref, v_ref, ci_ref, o_ref, co_ref, vtg_ref, kmean_ref, bias_ref):
    blk, grp = MOBA_BLOCK, MOBA_GROUP
    quad = grp * blk
    hps, seq, _ = k_ref.shape
    nb = seq // blk
    ngrp = nb // grp
    nt = (((1,), (1,)), ((), ()))

    co_ref[...] = ci_ref[...].astype(co_ref.dtype)

    def setup(h):
        for j in range(nb):
            kj = k_ref[h, j * blk:(j + 1) * blk, :].astype(F32)
            kmean_ref[h, j:j + 1, :] = jnp.sum(kj, axis=0, keepdims=True) * (1.0 / blk)
            vt = v_ref[h, j * blk:(j + 1) * blk, :].astype(F32).T.astype(BF16)
            vtg_ref[h, j // grp, 0:HEAD_DIM, (j % grp) * blk:(j % grp + 1) * blk] = vt
        ones_row = lax.broadcasted_iota(jnp.int32, (vtg_ref.shape[2] - HEAD_DIM, quad), 0) == 0
        for g in range(ngrp):
            vtg_ref[h, g, HEAD_DIM:, :] = jnp.where(ones_row, 1.0, 0.0).astype(BF16)
        gate = lax.dot_general(kmean_ref[h].astype(BF16), q_ref[h], nt, preferred_element_type=F32)
        row = lax.broadcasted_iota(jnp.int32, gate.shape, 0)
        qblk = lax.shift_right_logical(lax.broadcasted_iota(jnp.int32, gate.shape, 1), blk.bit_length() - 1)
        past = row < qblk
        gate = jnp.where(past, gate, -jnp.inf)
        rank = jnp.zeros(gate.shape, jnp.int32)
        for jp in range(nb):
            gj = gate[jp:jp + 1, :]
            beats = (gj > gate) | ((gj == gate) & (jp < row))
            rank = rank + beats.astype(jnp.int32)
        bias = jnp.where(past & (rank < MOBA_TOPK), 0.0, -jnp.inf).astype(F32)
        for g in range(ngrp):
            for qd in range(g, ngrp):
                bias_ref[h, g, qd, 0:grp, :] = bias[g * grp:(g + 1) * grp, qd * quad:(qd + 1) * quad]

    for h in range(hps):
        setup(h)

    kpos = lax.broadcasted_iota(jnp.int32, (blk, blk), 0)
    qpos = lax.broadcasted_iota(jnp.int32, (blk, blk), 1)
    causal = kpos <= qpos

    def scores(k_rows, q):
        return lax.dot_general(k_rows, q, nt, preferred_element_type=F32)

    def tiles_max(tiles):
        m = jnp.max(tiles[0], axis=0, keepdims=True)
        for t in tiles[1:]:
            m = jnp.maximum(m, jnp.max(t, axis=0, keepdims=True))
        return m

    heads = range(hps)

    def diagonal_group(qd, q0):
        s = [[scores(k_ref[h, pl.ds(q0, (r + 1) * blk), :],
                     q_ref[h, pl.ds(pl.multiple_of(q0 + r * blk, blk), blk), :]) for r in range(grp)]
             for h in heads]
        m, p = [], []
        for h in heads:
            mr, pr = [], []
            for r in range(grp):
                brow = bias_ref[h, qd, qd, 0:grp, r * blk:(r + 1) * blk]
                tiles = [s[h][r][t * blk:(t + 1) * blk] + brow[t:t + 1, :] for t in range(r)]
                tiles.append(jnp.where(causal, s[h][r][r * blk:(r + 1) * blk], -jnp.inf))
                mr.append(tiles_max(tiles))
                pr.append(jnp.concatenate([jnp.exp2(t - mr[r]) for t in tiles], axis=0).astype(BF16))
            m.append(jnp.concatenate(mr, axis=1))
            p.append(pr)
        acc = [jnp.concatenate(
            [jnp.dot(vtg_ref[h, qd, :, 0:(r + 1) * blk], p[h][r], preferred_element_type=F32)
             for r in range(grp)], axis=1) for h in heads]
        return tuple((m[h], acc[h]) for h in heads)

    def group_scores(q0, g):
        k0 = pl.multiple_of(g * quad, quad)
        return [scores(k_ref[h, pl.ds(k0, quad), :], q_ref[h, pl.ds(q0, quad), :]) for h in heads]

    def past_group_recentred(qd, q0, g, state):
        s = group_scores(q0, g)
        stats, p = [], []
        for h in heads:
            m = state[h][0]
            brow = bias_ref[h, g, qd, 0:grp, :]
            tiles = [s[h][t * blk:(t + 1) * blk] + brow[t:t + 1, :] for t in range(grp)]
            m_new = jnp.maximum(m, tiles_max(tiles))
            p.append(jnp.concatenate([jnp.exp2(t - m_new) for t in tiles], axis=0).astype(BF16))
            stats.append((m_new, jnp.exp2(m - m_new)))
        pv = [jnp.dot(vtg_ref[h, g], p[h], preferred_element_type=F32) for h in heads]
        return tuple((stats[h][0], stats[h][1] * state[h][1] + pv[h]) for h in heads)

    def past_group(qd, q0, g, state):
        s = group_scores(q0, g)
        p, top = [], []
        for h in heads:
            shift = bias_ref[h, g, qd, 0:grp, :] - state[h][0]
            z = [s[h][t * blk:(t + 1) * blk] + shift[t:t + 1, :] for t in range(grp)]
            top.append(tiles_max(z))
            p.append(jnp.concatenate([jnp.exp2(t) for t in z], axis=0).astype(BF16))
        pv = [jnp.dot(vtg_ref[h, g], p[h], preferred_element_type=F32) for h in heads]
        worst = top[0]
        for h in heads[1:]:
            worst = jnp.maximum(worst, top[h])
        return lax.cond(
            jnp.max(worst) <= SOFTMAX_HEADROOM,
            lambda: tuple((state[h][0], state[h][1] + pv[h]) for h in heads),
            lambda: past_group_recentred(qd, q0, g, state))

    def query_quad(qd, _):
        q0 = pl.multiple_of(qd * quad, quad)
        state = lax.fori_loop(0, qd, functools.partial(past_group, qd, q0), diagonal_group(qd, q0))
        for h, (_, acc) in enumerate(state):
            out = acc[0:HEAD_DIM] / acc[HEAD_DIM:HEAD_DIM + 1]
            o_ref[pl.ds(q0, quad), h * HEAD_DIM:(h + 1) * HEAD_DIM] = out.T.astype(o_ref.dtype)
        return 0

    lax.fori_loop(0, ngrp, query_quad, 0)


def _moba_attention(qk, v, w_cast, *, seq, n_heads):
    batch = qk.shape[0]
    blk, grp = MOBA_BLOCK, MOBA_GROUP
    hps = _pick(n_heads, MOBA_HEADS_PER_STEP)
    nb = seq // blk
    assert seq % (blk * grp) == 0 and blk & (blk - 1) == 0 and grp <= V7X_SUBLANES_F32
    heads = lambda off: pl.BlockSpec((None, hps, seq, HEAD_DIM), lambda b, h: (b, h + off, 0, 0))
    grid = (batch, n_heads // hps)
    cast_spec, cast_shape = _cast_stream(w_cast, grid)
    return pl.pallas_call(
        _moba_kernel,
        grid=grid,
        in_specs=[heads(0), heads(n_heads // hps), heads(0), cast_spec],
        out_specs=[pl.BlockSpec((None, seq, hps * HEAD_DIM), lambda b, h: (b, 0, h)), cast_spec],
        out_shape=[jax.ShapeDtypeStruct((batch, seq, n_heads * HEAD_DIM), BF16), cast_shape],
        scratch_shapes=[
            pltpu.VMEM((hps, nb // grp, HEAD_DIM + V7X_SUBLANES_BF16, grp * blk), BF16),
            pltpu.VMEM((hps, nb, HEAD_DIM), F32),
            pltpu.VMEM((hps, nb // grp, nb // grp, V7X_SUBLANES_F32, grp * blk), F32),
        ],
        compiler_params=_params("parallel", "parallel"),
        name="moba_attention",
    )(qk, qk, v, w_cast)


def _layer_norm_tiles(tile, nj, tn, gb_ref, o_ref, ob_ref=None):
    d = nj * tn
    tot = jnp.sum(tile(0), axis=1, keepdims=True)
    for jj in range(1, nj):
        tot = tot + jnp.sum(tile(jj), axis=1, keepdims=True)
    mu = tot * (1.0 / d)
    sq = jnp.sum(jnp.square(tile(0) - mu), axis=1, keepdims=True)
    for jj in range(1, nj):
        sq = sq + jnp.sum(jnp.square(tile(jj) - mu), axis=1, keepdims=True)
    rstd = lax.rsqrt(sq * (1.0 / d) + LN_EPS)
    for jj in range(nj):
        cs = slice(jj * tn, (jj + 1) * tn)
        y = (tile(jj) - mu) * rstd * gb_ref[0:1, cs] + gb_ref[1:2, cs]
        o_ref[:, cs] = y
        if ob_ref is not None:
            ob_ref[:, cs] = y.astype(ob_ref.dtype)


def _outproj_ln_kernel(a_ref, p_ref, w_ref, x_ref, gb_ref, o_ref, ob_ref, *, alpha, tn):
    ka = a_ref.shape[1]
    nj = o_ref.shape[1] // tn
    for jj in range(nj):
        cs = slice(jj * tn, (jj + 1) * tn)
        mix = jnp.dot(a_ref[...], w_ref[0:ka, cs], preferred_element_type=F32)
        mix = mix + jnp.dot(p_ref[...], w_ref[ka:, cs], preferred_element_type=F32)
        o_ref[:, cs] = alpha * x_ref[:, cs] + mix
    _layer_norm_tiles(lambda jj: o_ref[:, jj * tn:(jj + 1) * tn], nj, tn, gb_ref, o_ref, ob_ref)


def _outproj_ln(attn, pool, w_out, x2, gb, *, alpha):
    T, D = x2.shape
    ka, kp = attn.shape[1], pool.shape[1]
    tm = _pick(T, 256)
    return pl.pallas_call(
        functools.partial(_outproj_ln_kernel, alpha=alpha, tn=_pick(D, OUTPROJ_COL_TILE)),
        grid=(T // tm,),
        in_specs=[
            pl.BlockSpec((tm, ka), lambda i: (i, 0)),
            pl.BlockSpec((tm, kp), lambda i: (i, 0)),
            pl.BlockSpec((ka + kp, D), lambda i: (0, 0), pipeline_mode=pl.Buffered(1)),
            pl.BlockSpec((tm, D), lambda i: (i, 0)),
            pl.BlockSpec((2, D), lambda i: (0, 0)),
        ],
        out_specs=[pl.BlockSpec((tm, D), lambda i: (i, 0))] * 2,
        out_shape=[jax.ShapeDtypeStruct((T, D), F32), jax.ShapeDtypeStruct((T, D), BF16)],
        compiler_params=_params("parallel"),
        name="outproj_ln1",
    )(attn, pool, w_out, x2, gb)


def _ffn_ln_kernel(hb_ref, h_hbm, w1_ref, w2_ref, gb_ref, o_ref, res_ref, res_sem, *, alpha, tn):
    f = pl.program_id(1)
    tm = o_ref.shape[0]
    nj = o_ref.shape[1] // tn
    rows = pl.ds(pl.multiple_of(pl.program_id(0) * tm, tm), tm)
    residual_copy = pltpu.make_async_copy(h_hbm.at[rows, :], res_ref, res_sem)

    @pl.when(f == 0)
    def _():
        residual_copy.start()
        o_ref[...] = jnp.zeros_like(o_ref)

    hid = jnp.dot(hb_ref[...], w1_ref[...], preferred_element_type=F32)
    hid = jnp.square(jnp.maximum(hid, 0.0)).astype(BF16)
    for jj in range(nj):
        cs = slice(jj * tn, (jj + 1) * tn)
        o_ref[:, cs] += jnp.dot(hid, w2_ref[:, cs], preferred_element_type=F32)

    @pl.when(f == pl.num_programs(1) - 1)
    def _():
        residual_copy.wait()
        o_ref[...] = alpha * res_ref[...] + o_ref[...]
        _layer_norm_tiles(lambda jj: o_ref[:, jj * tn:(jj + 1) * tn], nj, tn, gb_ref, o_ref)


def _ffn_ln(hb, h, w1, w2, gb, *, alpha):
    T, D = h.shape
    F = w1.shape[1]
    tm = _pick(T, 512)
    tf = _pick(F, FFN_HIDDEN_TILE)
    tn = _pick(D, 512)
    return pl.pallas_call(
        functools.partial(_ffn_ln_kernel, alpha=alpha, tn=tn),
        grid=(T // tm, F // tf),
        in_specs=[
            pl.BlockSpec((tm, D), lambda i, f: (i, 0)),
            pl.BlockSpec(memory_space=pl.ANY),
            pl.BlockSpec((D, tf), lambda i, f: (0, f)),
            pl.BlockSpec((tf, D), lambda i, f: (f, 0)),
            pl.BlockSpec((2, D), lambda i, f: (0, 0)),
        ],
        out_specs=pl.BlockSpec((tm, D), lambda i, f: (i, 0)),
        out_shape=jax.ShapeDtypeStruct((T, D), F32),
        scratch_shapes=[pltpu.VMEM((tm, D), F32), pltpu.SemaphoreType.DMA(())],
        compiler_params=_params("parallel", "arbitrary"),
        name="ffn_ln2",
    )(hb, h, w1, w2, gb)


def _rope_table(seq, q_scale):
    inv_freq = 1.0 / (ROPE_THETA ** (jnp.arange(0, HEAD_DIM, 2, dtype=F32) / HEAD_DIM))
    ang = jnp.arange(seq, dtype=F32)[:, None] * inv_freq[None, :]
    cos, sin = jnp.cos(ang), jnp.sin(ang)
    cos = jnp.concatenate([cos, cos], axis=-1)
    sin = jnp.concatenate([-sin, sin], axis=-1)
    return jnp.stack([jnp.stack([q_scale * cos, q_scale * sin]), jnp.stack([cos, sin])])


def kernel(x, w_in, w_pool, pool_scale, w_out, ln1_g, ln1_b, w_ff1, w_ff2, ln2_g, ln2_b):
    B, S, D = x.shape
    depth = w_in.shape[0]
    attn_w = D // 2
    n_heads = attn_w // HEAD_DIM
    pool_w = D - attn_w
    alpha = (2.0 * depth) ** 0.25
    rope_table = _rope_table(S, HEAD_DIM ** -0.5 * LOG2_E)

    x2 = x.reshape(B * S, D)
    for l in range(depth):
        w_vu_b = w_in[l][:, 2 * attn_w:].astype(BF16)
        pool, xb = _project_pool(x2, w_vu_b, w_pool[l].astype(BF16), pool_scale[l].reshape(1, pool_w),
                                 seq=S, col0=attn_w)
        v, w_out_b, w_qk_b = _project_heads(xb, w_vu_b, [(w_out[l], None), (w_in[l], 2 * attn_w)],
                                            batch=B, seq=S, col0=0, ncols=attn_w)
        qk, w_ff1_b = _project_heads(xb, w_qk_b, [(w_ff1[l], None)], batch=B, seq=S, col0=0,
                                     ncols=2 * attn_w, rope_table=rope_table)

        attn, w_ff2_b = _moba_attention(qk, v, w_ff2[l], seq=S, n_heads=n_heads)
        h, hb = _outproj_ln(attn.reshape(B * S, attn_w), pool, w_out_b, x2,
                            jnp.stack([ln1_g[l], ln1_b[l]]), alpha=alpha)
        x2 = _ffn_ln(hb, h, w_ff1_b, w_ff2_b, jnp.stack([ln2_g[l], ln2_b[l]]), alpha=alpha)
    return x2.reshape(B, S, D)
```

```python
import functools

import jax
import jax.numpy as jnp
from jax import lax
from jax.experimental import pallas as pl
from jax.experimental.pallas import tpu as pltpu

HEAD_DIM = 128
POOL_WINDOWS = (2, 4, 8, 16)
MOBA_BLOCK = 256
MOBA_TOPK = 3
MOBA_GROUP = 4
MOBA_HEADS_PER_STEP = 2
SOFTMAX_HEADROOM = 64.0
LOG2_E = 1.4426950408889634
FFN_HIDDEN_TILE = 1024
OUTPROJ_COL_TILE = 1024
ROPE_THETA = 10000.0
LN_EPS = 1e-5
POOL_HALO = 16

V7X_SUBLANES_F32 = 8
V7X_SUBLANES_BF16 = 16
V7X_VMEM_LIMIT_BYTES = 62 * 1024 * 1024

F32 = jnp.float32
BF16 = jnp.bfloat16


def _pick(n, pref):
    t = min(pref, n)
    while n % t:
        t //= 2
    return t


def _params(*sem):
    return pltpu.CompilerParams(dimension_semantics=sem, vmem_limit_bytes=V7X_VMEM_LIMIT_BYTES)


def _proj_pool_kernel(x_ref, w_ref, wp_ref, scale_ref, pool_ref, xb_ref, ext_ref, *, tiles_per_seq):
    tm = x_ref.shape[0]
    cg = wp_ref.shape[1]
    s = pl.program_id(0) % tiles_per_seq
    xb = x_ref[...].astype(xb_ref.dtype)
    xb_ref[...] = xb

    @pl.when(s == 0)
    def _():
        ext_ref[0:POOL_HALO, :] = jnp.zeros((POOL_HALO, ext_ref.shape[1]), F32)

    @pl.when(s != 0)
    def _():
        ext_ref[0:POOL_HALO, :] = ext_ref[tm:tm + POOL_HALO, :]

    pos1 = lax.broadcasted_iota(jnp.int32, (tm, cg), 0) + (s * tm + 1)
    proj = lambda g: jnp.dot(xb, w_ref[:, g * cg:(g + 1) * cg], preferred_element_type=F32)
    order = sorted(range(len(POOL_WINDOWS)), key=lambda g: -POOL_WINDOWS[g])
    u_next = proj(order[0])
    for k, g in enumerate(order):
        w = POOL_WINDOWS[g]
        cs = slice(g * cg, (g + 1) * cg)
        ug = u_next
        if k + 1 < len(order):
            u_next = proj(order[k + 1])
        ext_ref[POOL_HALO:, cs] = ug
        tot = ug
        for d in range(1, w):
            tot = tot + ext_ref[POOL_HALO - d:POOL_HALO - d + tm, cs]
        count = jnp.minimum(pos1, w).astype(F32)
        pooled = tot / count - ug
        out = jnp.dot(pooled.astype(BF16), wp_ref[g], preferred_element_type=F32) * scale_ref[:, cs]
        pool_ref[:, cs] = out.astype(pool_ref.dtype)


def _proj_heads_kernel(*refs, rope, n_cast):
    x_ref, w_ref = refs[:2]
    rope_ref = refs[2] if rope else None
    ci_refs = refs[len(refs) - 2 * n_cast - 1:len(refs) - n_cast - 1]
    o_ref = refs[len(refs) - n_cast - 1]
    co_refs = refs[len(refs) - n_cast:]
    for ci_ref, co_ref in zip(ci_refs, co_refs):
        co_ref[...] = ci_ref[...].astype(co_ref.dtype)
    acc = jnp.dot(x_ref[...], w_ref[...], preferred_element_type=F32)
    for hh in range(o_ref.shape[0]):
        t = acc[:, hh * HEAD_DIM:(hh + 1) * HEAD_DIM]
        if rope:
            t = t * rope_ref[0] + pltpu.roll(t, HEAD_DIM // 2, 1) * rope_ref[1]
        o_ref[hh] = t.astype(o_ref.dtype)


def _cast_stream(w_f32, grid, ncols=None):
    R = w_f32.shape[0]
    C = w_f32.shape[1] if ncols is None else ncols
    assert C % HEAD_DIM == 0
    steps = grid[0] * grid[1]
    rb = R // steps
    assert R % steps == 0 and rb % V7X_SUBLANES_BF16 == 0, (R, steps)
    spec = pl.BlockSpec((rb, C), lambda i, j: (i * grid[1] + j, 0))
    return spec, jax.ShapeDtypeStruct((R, C), BF16)


def _project_pool(x2, w, w_pool, pool_scale, *, seq, col0):
    T, D = x2.shape
    G, cg, _ = w_pool.shape
    P = G * cg
    tm = _pick(seq, 512)
    assert tm >= POOL_HALO >= max(POOL_WINDOWS) - 1 and col0 % P == 0
    return pl.pallas_call(
        functools.partial(_proj_pool_kernel, tiles_per_seq=seq // tm),
        grid=(T // tm,),
        in_specs=[
            pl.BlockSpec((tm, D), lambda i: (i, 0)),
            pl.BlockSpec((D, P), lambda i: (0, col0 // P), pipeline_mode=pl.Buffered(1)),
            pl.BlockSpec((G, cg, cg), lambda i: (0, 0, 0)),
            pl.BlockSpec((1, P), lambda i: (0, 0)),
        ],
        out_specs=[pl.BlockSpec((tm, P), lambda i: (i, 0)), pl.BlockSpec((tm, D), lambda i: (i, 0))],
        out_shape=[jax.ShapeDtypeStruct((T, P), BF16), jax.ShapeDtypeStruct((T, D), BF16)],
        scratch_shapes=[pltpu.VMEM((POOL_HALO + tm, P), F32)],
        compiler_params=_params("arbitrary"),
        name="proj_pool",
    )(x2, w, w_pool, pool_scale)


def _project_heads(xb, w, w_casts, *, batch, seq, col0, ncols, rope_table=None):
    T, D = xb.shape
    tm = _pick(seq, 1024)
    tn = _pick(ncols // 2, 1024)
    spt = seq // tm
    hpt = tn // HEAD_DIM
    j0 = col0 // tn
    grid = (T // tm, ncols // tn)
    half = grid[1] // 2
    cast_specs, cast_shapes = zip(*(_cast_stream(a, grid, n) for a, n in w_casts))
    table_spec = pl.BlockSpec((None, 2, tm, HEAD_DIM), lambda i, j: (j // half, 0, i % spt, 0))
    tables = () if rope_table is None else (rope_table,)
    return pl.pallas_call(
        functools.partial(_proj_heads_kernel, rope=rope_table is not None, n_cast=len(w_casts)),
        grid=grid,
        in_specs=[
            pl.BlockSpec((tm, D), lambda i, j: (i, 0)),
            pl.BlockSpec((D, tn), lambda i, j: (0, j + j0)),
            *([table_spec] * len(tables)),
            *cast_specs,
        ],
        out_specs=[
            pl.BlockSpec((None, hpt, tm, HEAD_DIM), lambda i, j: (i // spt, j, i % spt, 0)),
            *cast_specs,
        ],
        out_shape=[jax.ShapeDtypeStruct((batch, ncols // HEAD_DIM, seq, HEAD_DIM), BF16), *cast_shapes],
        compiler_params=_params("parallel", "arbitrary"),
        name="proj_heads" if rope_table is None else "proj_rope_heads",
    )(xb, w, *tables, *(a for a, _ in w_casts))


def _moba_kernel(q_ref, k_ref, v_ref, ci_ref, o_ref, co_ref, vtg_ref, kmean_ref, bias_ref):
    blk, grp = MOBA_BLOCK, MOBA_GROUP
    quad = grp * blk
    hps, seq, _ = k_ref.shape
    nb = seq // blk
    ngrp = nb // grp
    nt = (((1,), (1,)), ((), ()))

    co_ref[...] = ci_ref[...].astype(co_ref.dtype)

    def setup(h):
        for j in range(nb):
            kj = k_ref[h, j * blk:(j + 1) * blk, :].astype(F32)
            kmean_ref[h, j:j + 1, :] = jnp.sum(kj, axis=0, keepdims=True) * (1.0 / blk)
            vt = v_ref[h, j * blk:(j + 1) * blk, :].astype(F32).T.astype(BF16)
            vtg_ref[h, j // grp, 0:HEAD_DIM, (j % grp) * blk:(j % grp + 1) * blk] = vt
        ones_row = lax.broadcasted_iota(jnp.int32, (vtg_ref.shape[2] - HEAD_DIM, quad), 0) == 0
        for g in range(ngrp):
            vtg_ref[h, g, HEAD_DIM:, :] = jnp.where(ones_row, 1.0, 0.0).astype(BF16)
        gate = lax.dot_general(kmean_ref[h].astype(BF16), q_ref[h], nt, preferred_element_type=F32)
        row = lax.broadcasted_iota(jnp.int32, gate.shape, 0)
        qblk = lax.shift_right_logical(lax.broadcasted_iota(jnp.int32, gate.shape, 1), blk.bit_length() - 1)
        past = row < qblk
        gate = jnp.where(past, gate, -jnp.inf)
        rank = jnp.zeros(gate.shape, jnp.int32)
        for jp in range(nb):
            gj = gate[jp:jp + 1, :]
            beats = (gj > gate) | ((gj == gate) & (jp < row))
            rank = rank + beats.astype(jnp.int32)
        bias = jnp.where(past & (rank < MOBA_TOPK), 0.0, -jnp.inf).astype(F32)
        for g in range(ngrp):
            for qd in range(g, ngrp):
                bias_ref[h, g, qd, 0:grp, :] = bias[g * grp:(g + 1) * grp, qd * quad:(qd + 1) * quad]

    for h in range(hps):
        setup(h)

    kpos = lax.broadcasted_iota(jnp.int32, (blk, blk), 0)
    qpos = lax.broadcasted_iota(jnp.int32, (blk, blk), 1)
    causal = kpos <= qpos

    def scores(k_rows, q):
        return lax.dot_general(k_rows, q, nt, preferred_element_type=F32)

    def tiles_max(tiles):
        m = jnp.max(tiles[0], axis=0, keepdims=True)
        for t in tiles[1:]:
            m = jnp.maximum(m, jnp.max(t, axis=0, keepdims=True))
        return m

    heads = range(hps)

    def diagonal_group(qd, q0):
        s = [[scores(k_ref[h, pl.ds(q0, (r + 1) * blk), :],
                     q_ref[h, pl.ds(pl.multiple_of(q0 + r * blk, blk), blk), :]) for r in range(grp)]
             for h in heads]
        m, p = [], []
        for h in heads:
            mr, pr = [], []
            for r in range(grp):
                brow = bias_ref[h, qd, qd, 0:grp, r * blk:(r + 1) * blk]
                tiles = [s[h][r][t * blk:(t + 1) * blk] + brow[t:t + 1, :] for t in range(r)]
                tiles.append(jnp.where(causal, s[h][r][r * blk:(r + 1) * blk], -jnp.inf))
                mr.append(tiles_max(tiles))
                pr.append(jnp.concatenate([jnp.exp2(t - mr[r]) for t in tiles], axis=0).astype(BF16))
            m.append(jnp.concatenate(mr, axis=1))
            p.append(pr)
        acc = [jnp.concatenate(
            [jnp.dot(vtg_ref[h, qd, :, 0:(r + 1) * blk], p[h][r], preferred_element_type=F32)
             for r in range(grp)], axis=1) for h in heads]
        return tuple((m[h], acc[h]) for h in heads)

    def group_scores(q0, g):
        k0 = pl.multiple_of(g * quad, quad)
        return [scores(k_ref[h, pl.ds(k0, quad), :], q_ref[h, pl.ds(q0, quad), :]) for h in heads]

    def past_group_recentred(qd, q0, g, state):
        s = group_scores(q0, g)
        stats, p = [], []
        for h in heads:
            m = state[h][0]
            brow = bias_ref[h, g, qd, 0:grp, :]
            tiles = [s[h][t * blk:(t + 1) * blk] + brow[t:t + 1, :] for t in range(grp)]
            m_new = jnp.maximum(m, tiles_max(tiles))
            p.append(jnp.concatenate([jnp.exp2(t - m_new) for t in tiles], axis=0).astype(BF16))
            stats.append((m_new, jnp.exp2(m - m_new)))
        pv = [jnp.dot(vtg_ref[h, g], p[h], preferred_element_type=F32) for h in heads]
        return tuple((stats[h][0], stats[h][1] * state[h][1] + pv[h]) for h in heads)

    def past_group(qd, q0, g, state):
        s = group_scores(q0, g)
        p, top = [], []
        for h in heads:
            shift = bias_ref[h, g, qd, 0:grp, :] - state[h][0]
            z = [s[h][t * blk:(t + 1) * blk] + shift[t:t + 1, :] for t in range(grp)]
            top.append(tiles_max(z))
            p.append(jnp.concatenate([jnp.exp2(t) for t in z], axis=0).astype(BF16))
        pv = [jnp.dot(vtg_ref[h, g], p[h], preferred_element_type=F32) for h in heads]
        worst = top[0]
        for h in heads[1:]:
            worst = jnp.maximum(worst, top[h])
        return lax.cond(
            jnp.max(worst) <= SOFTMAX_HEADROOM,
            lambda: tuple((state[h][0], state[h][1] + pv[h]) for h in heads),
            lambda: past_group_recentred(qd, q0, g, state))

    def query_quad(qd, _):
        q0 = pl.multiple_of(qd * quad, quad)
        state = lax.fori_loop(0, qd, functools.partial(past_group, qd, q0), diagonal_group(qd, q0))
        for h, (_, acc) in enumerate(state):
            out = acc[0:HEAD_DIM] / acc[HEAD_DIM:HEAD_DIM + 1]
            o_ref[pl.ds(q0, quad), h * HEAD_DIM:(h + 1) * HEAD_DIM] = out.T.astype(o_ref.dtype)
        return 0

    lax.fori_loop(0, ngrp, query_quad, 0)


def _moba_attention(qk, v, w_cast, *, seq, n_heads):
    batch = qk.shape[0]
    blk, grp = MOBA_BLOCK, MOBA_GROUP
    hps = _pick(n_heads, MOBA_HEADS_PER_STEP)
    nb = seq // blk
    assert seq % (blk * grp) == 0 and blk & (blk - 1) == 0 and grp <= V7X_SUBLANES_F32
    heads = lambda off: pl.BlockSpec((None, hps, seq, HEAD_DIM), lambda b, h: (b, h + off, 0, 0))
    grid = (batch, n_heads // hps)
    cast_spec, cast_shape = _cast_stream(w_cast, grid)
    return pl.pallas_call(
        _moba_kernel,
        grid=grid,
        in_specs=[heads(0), heads(n_heads // hps), heads(0), cast_spec],
        out_specs=[pl.BlockSpec((None, seq, hps * HEAD_DIM), lambda b, h: (b, 0, h)), cast_spec],
        out_shape=[jax.ShapeDtypeStruct((batch, seq, n_heads * HEAD_DIM), BF16), cast_shape],
        scratch_shapes=[
            pltpu.VMEM((hps, nb // grp, HEAD_DIM + V7X_SUBLANES_BF16, grp * blk), BF16),
            pltpu.VMEM((hps, nb, HEAD_DIM), F32),
            pltpu.VMEM((hps, nb // grp, nb // grp, V7X_SUBLANES_F32, grp * blk), F32),
        ],
        compiler_params=_params("parallel", "parallel"),
        name="moba_attention",
    )(qk, qk, v, w_cast)


def _layer_norm_tiles(tile, nj, tn, gb_ref, o_ref, ob_ref=None, o_scale=None):
    d = nj * tn
    tot = jnp.sum(tile(0), axis=1, keepdims=True)
    for jj in range(1, nj):
        tot = tot + jnp.sum(tile(jj), axis=1, keepdims=True)
    mu = tot * (1.0 / d)
    sq = jnp.sum(jnp.square(tile(0) - mu), axis=1, keepdims=True)
    for jj in range(1, nj):
        sq = sq + jnp.sum(jnp.square(tile(jj) - mu), axis=1, keepdims=True)
    rstd = lax.rsqrt(sq * (1.0 / d) + LN_EPS)
    for jj in range(nj):
        cs = slice(jj * tn, (jj + 1) * tn)
        y = (tile(jj) - mu) * rstd * gb_ref[0:1, cs] + gb_ref[1:2, cs]
        o_ref[:, cs] = y if o_scale is None else o_scale * y
        if ob_ref is not None:
            ob_ref[:, cs] = y.astype(ob_ref.dtype)


def _outproj_ln_kernel(a_ref, p_ref, w_ref, x_ref, gb_ref, o_ref, ob_ref, *, alpha, tn):
    ka = a_ref.shape[1]
    nj = o_ref.shape[1] // tn
    for jj in range(nj):
        cs = slice(jj * tn, (jj + 1) * tn)
        mix = jnp.dot(a_ref[...], w_ref[0:ka, cs], preferred_element_type=F32)
        mix = mix + jnp.dot(p_ref[...], w_ref[ka:, cs], preferred_element_type=F32)
        o_ref[:, cs] = alpha * x_ref[:, cs] + mix
    _layer_norm_tiles(lambda jj: o_ref[:, jj * tn:(jj + 1) * tn], nj, tn, gb_ref, o_ref, ob_ref, alpha)


def _outproj_ln(attn, pool, w_out, x2, gb, *, alpha):
    T, D = x2.shape
    ka, kp = attn.shape[1], pool.shape[1]
    tm = _pick(T, 256)
    return pl.pallas_call(
        functools.partial(_outproj_ln_kernel, alpha=alpha, tn=_pick(D, OUTPROJ_COL_TILE)),
        grid=(T // tm,),
        in_specs=[
            pl.BlockSpec((tm, ka), lambda i: (i, 0)),
            pl.BlockSpec((tm, kp), lambda i: (i, 0)),
            pl.BlockSpec((ka + kp, D), lambda i: (0, 0), pipeline_mode=pl.Buffered(1)),
            pl.BlockSpec((tm, D), lambda i: (i, 0)),
            pl.BlockSpec((2, D), lambda i: (0, 0)),
        ],
        out_specs=[pl.BlockSpec((tm, D), lambda i: (i, 0))] * 2,
        out_shape=[jax.ShapeDtypeStruct((T, D), F32), jax.ShapeDtypeStruct((T, D), BF16)],
        compiler_params=_params("parallel"),
        name="outproj_ln1",
    )(attn, pool, w_out, x2, gb)


def _ffn_ln_kernel(hb_ref, ah_hbm, w1_ref, w2_ref, gb_ref, y_hbm, acc_ref, init_sem, out_sem, *, tn):
    i, f = pl.program_id(0), pl.program_id(1)
    last_i, last_f = pl.num_programs(0) - 1, pl.num_programs(1) - 1
    tm = acc_ref.shape[0]
    nj = acc_ref.shape[1] // tn
    tile_rows = lambda t: pl.ds(pl.multiple_of(t * tm, tm), tm)
    residual_copy = pltpu.make_async_copy(ah_hbm.at[tile_rows(i), :], acc_ref, init_sem)
    result_copy = lambda t: pltpu.make_async_copy(acc_ref, y_hbm.at[tile_rows(t), :], out_sem)

    @pl.when(f == 0)
    def _():
        @pl.when(i > 0)
        def _():
            result_copy(i - 1).wait()
        residual_copy.start()

    hid = jnp.dot(hb_ref[...], w1_ref[...], preferred_element_type=F32)
    hid = jnp.square(jnp.maximum(hid, 0.0)).astype(BF16)

    @pl.when(f == 0)
    def _():
        residual_copy.wait()

    for jj in range(nj):
        cs = slice(jj * tn, (jj + 1) * tn)
        acc_ref[:, cs] += jnp.dot(hid, w2_ref[:, cs], preferred_element_type=F32)

    @pl.when(f == last_f)
    def _():
        _layer_norm_tiles(lambda jj: acc_ref[:, jj * tn:(jj + 1) * tn], nj, tn, gb_ref, acc_ref)
        result_copy(i).start()

        @pl.when(i == last_i)
        def _():
            result_copy(i).wait()


def _ffn_ln(hb, ah, w1, w2, gb):
    T, D = ah.shape
    F = w1.shape[1]
    tm = _pick(T, 512)
    tf = _pick(F, FFN_HIDDEN_TILE)
    tn = _pick(D, 512)
    return pl.pallas_call(
        functools.partial(_ffn_ln_kernel, tn=tn),
        grid=(T // tm, F // tf),
        in_specs=[
            pl.BlockSpec((tm, D), lambda i, f: (i, 0)),
            pl.BlockSpec(memory_space=pl.ANY),
            pl.BlockSpec((D, tf), lambda i, f: (0, f)),
            pl.BlockSpec((tf, D), lambda i, f: (f, 0)),
            pl.BlockSpec((2, D), lambda i, f: (0, 0)),
        ],
        out_specs=pl.BlockSpec(memory_space=pl.ANY),
        out_shape=jax.ShapeDtypeStruct((T, D), F32),
        scratch_shapes=[pltpu.VMEM((tm, D), F32), pltpu.SemaphoreType.DMA(()), pltpu.SemaphoreType.DMA(())],
        compiler_params=_params("arbitrary", "arbitrary"),
        name="ffn_ln2",
    )(hb, ah, w1, w2, gb)


def _rope_table(seq, q_scale):
    inv_freq = 1.0 / (ROPE_THETA ** (jnp.arange(0, HEAD_DIM, 2, dtype=F32) / HEAD_DIM))
    ang = jnp.arange(seq, dtype=F32)[:, None] * inv_freq[None, :]
    cos, sin = jnp.cos(ang), jnp.sin(ang)
    cos = jnp.concatenate([cos, cos], axis=-1)
    sin = jnp.concatenate([-sin, sin], axis=-1)
    return jnp.stack([jnp.stack([q_scale * cos, q_scale * sin]), jnp.stack([cos, sin])])


def kernel(x, w_in, w_pool, pool_scale, w_out, ln1_g, ln1_b, w_ff1, w_ff2, ln2_g, ln2_b):
    B, S, D = x.shape
    depth = w_in.shape[0]
    attn_w = D // 2
    n_heads = attn_w // HEAD_DIM
    pool_w = D - attn_w
    alpha = (2.0 * depth) ** 0.25
    rope_table = _rope_table(S, HEAD_DIM ** -0.5 * LOG2_E)

    x2 = x.reshape(B * S, D)
    for l in range(depth):
        w_vu_b = w_in[l][:, 2 * attn_w:].astype(BF16)
        pool, xb = _project_pool(x2, w_vu_b, w_pool[l].astype(BF16), pool_scale[l].reshape(1, pool_w),
                                 seq=S, col0=attn_w)
        v, w_out_b, w_qk_b = _project_heads(xb, w_vu_b, [(w_out[l], None), (w_in[l], 2 * attn_w)],
                                            batch=B, seq=S, col0=0, ncols=attn_w)
        qk, w_ff1_b = _project_heads(xb, w_qk_b, [(w_ff1[l], None)], batch=B, seq=S, col0=0,
                                     ncols=2 * attn_w, rope_table=rope_table)

        attn, w_ff2_b = _moba_attention(qk, v, w_ff2[l], seq=S, n_heads=n_heads)
        ah, hb = _outproj_ln(attn.reshape(B * S, attn_w), pool, w_out_b, x2,
                             jnp.stack([ln1_g[l], ln1_b[l]]), alpha=alpha)
        x2 = _ffn_ln(hb, ah, w_ff1_b, w_ff2_b, jnp.stack([ln2_g[l], ln2_b[l]]))
    return x2.reshape(B, S, D)
```

```python
import functools

import jax
import jax.numpy as jnp
from jax import lax
from jax.experimental import pallas as pl
from jax.experimental.pallas import tpu as pltpu

HEAD_DIM = 128
POOL_WINDOWS = (2, 4, 8, 16)
MOBA_BLOCK = 256
MOBA_TOPK = 3
MOBA_GROUP = 4
MOBA_HEADS_PER_STEP = 2
SOFTMAX_HEADROOM = 64.0
LOG2_E = 1.4426950408889634
FFN_HIDDEN_TILE = 1024
OUTPROJ_COL_TILE = 1024
ROPE_THETA = 10000.0
LN_EPS = 1e-5
POOL_HALO = 16

V7X_SUBLANES_F32 = 8
V7X_SUBLANES_BF16 = 16
V7X_VMEM_LIMIT_BYTES = 62 * 1024 * 1024

F32 = jnp.float32
BF16 = jnp.bfloat16


def _pick(n, pref):
    t = min(pref, n)
    while n % t:
        t //= 2
    return t


def _params(*sem):
    return pltpu.CompilerParams(dimension_semantics=sem, vmem_limit_bytes=V7X_VMEM_LIMIT_BYTES)


def _proj_pool_kernel(x_ref, w_ref, wp_ref, scale_ref, pool_ref, xb_ref, ext_ref, *, tiles_per_seq):
    tm = x_ref.shape[0]
    cg = wp_ref.shape[1]
    s = pl.program_id(0) % tiles_per_seq
    xb = x_ref[...].astype(xb_ref.dtype)
    xb_ref[...] = xb

    @pl.when(s == 0)
    def _():
        ext_ref[0:POOL_HALO, :] = jnp.zeros((POOL_HALO, ext_ref.shape[1]), F32)

    @pl.when(s != 0)
    def _():
        ext_ref[0:POOL_HALO, :] = ext_ref[tm:tm + POOL_HALO, :]

    pos1 = lax.broadcasted_iota(jnp.int32, (tm, cg), 0) + (s * tm + 1)
    proj = lambda g: jnp.dot(xb, w_ref[:, g * cg:(g + 1) * cg], preferred_element_type=F32)
    order = sorted(range(len(POOL_WINDOWS)), key=lambda g: -POOL_WINDOWS[g])
    u_next = proj(order[0])
    for k, g in enumerate(order):
        w = POOL_WINDOWS[g]
        cs = slice(g * cg, (g + 1) * cg)
        ug = u_next
        if k + 1 < len(order):
            u_next = proj(order[k + 1])
        ext_ref[POOL_HALO:, cs] = ug
        tot = ug
        for d in range(1, w):
            tot = tot + ext_ref[POOL_HALO - d:POOL_HALO - d + tm, cs]
        count = jnp.minimum(pos1, w).astype(F32)
        pooled = tot / count - ug
        out = jnp.dot(pooled.astype(BF16), wp_ref[g], preferred_element_type=F32) * scale_ref[:, cs]
        pool_ref[:, cs] = out.astype(pool_ref.dtype)


def _proj_heads_kernel(*refs, rope, n_cast):
    x_ref, w_ref = refs[:2]
    rope_ref = refs[2] if rope else None
    ci_refs = refs[len(refs) - 2 * n_cast - 1:len(refs) - n_cast - 1]
    o_ref = refs[len(refs) - n_cast - 1]
    co_refs = refs[len(refs) - n_cast:]
    for ci_ref, co_ref in zip(ci_refs, co_refs):
        co_ref[...] = ci_ref[...].astype(co_ref.dtype)
    acc = jnp.dot(x_ref[...], w_ref[...], preferred_element_type=F32)
    for hh in range(o_ref.shape[0]):
        t = acc[:, hh * HEAD_DIM:(hh + 1) * HEAD_DIM]
        if rope:
            t = t * rope_ref[0] + pltpu.roll(t, HEAD_DIM // 2, 1) * rope_ref[1]
        o_ref[hh] = t.astype(o_ref.dtype)


def _cast_stream(w_f32, grid, ncols=None):
    R = w_f32.shape[0]
    C = w_f32.shape[1] if ncols is None else ncols
    assert C % HEAD_DIM == 0
    steps = grid[0] * grid[1]
    rb = R // steps
    assert R % steps == 0 and rb % V7X_SUBLANES_BF16 == 0, (R, steps)
    spec = pl.BlockSpec((rb, C), lambda i, j: (i * grid[1] + j, 0))
    return spec, jax.ShapeDtypeStruct((R, C), BF16)


def _project_pool(x2, w, w_pool, pool_scale, *, seq, col0):
    T, D = x2.shape
    G, cg, _ = w_pool.shape
    P = G * cg
    tm = _pick(seq, 512)
    assert tm >= POOL_HALO >= max(POOL_WINDOWS) - 1 and col0 % P == 0
    return pl.pallas_call(
        functools.partial(_proj_pool_kernel, tiles_per_seq=seq // tm),
        grid=(T // tm,),
        in_specs=[
            pl.BlockSpec((tm, D), lambda i: (i, 0)),
            pl.BlockSpec((D, P), lambda i: (0, col0 // P), pipeline_mode=pl.Buffered(1)),
            pl.BlockSpec((G, cg, cg), lambda i: (0, 0, 0)),
            pl.BlockSpec((1, P), lambda i: (0, 0)),
        ],
        out_specs=[pl.BlockSpec((tm, P), lambda i: (i, 0)), pl.BlockSpec((tm, D), lambda i: (i, 0))],
        out_shape=[jax.ShapeDtypeStruct((T, P), BF16), jax.ShapeDtypeStruct((T, D), BF16)],
        scratch_shapes=[pltpu.VMEM((POOL_HALO + tm, P), F32)],
        compiler_params=_params("arbitrary"),
        name="proj_pool",
    )(x2, w, w_pool, pool_scale)


def _project_heads(xb, w, w_casts, *, batch, seq, col0, ncols, rope_table=None):
    T, D = xb.shape
    tm = _pick(seq, 1024)
    tn = _pick(ncols // 2, 1024)
    spt = seq // tm
    hpt = tn // HEAD_DIM
    j0 = col0 // tn
    grid = (T // tm, ncols // tn)
    half = grid[1] // 2
    cast_specs, cast_shapes = zip(*(_cast_stream(a, grid, n) for a, n in w_casts))
    table_spec = pl.BlockSpec((None, 2, tm, HEAD_DIM), lambda i, j: (j // half, 0, i % spt, 0))
    tables = () if rope_table is None else (rope_table,)
    return pl.pallas_call(
        functools.partial(_proj_heads_kernel, rope=rope_table is not None, n_cast=len(w_casts)),
        grid=grid,
        in_specs=[
            pl.BlockSpec((tm, D), lambda i, j: (i, 0)),
            pl.BlockSpec((D, tn), lambda i, j: (0, j + j0)),
            *([table_spec] * len(tables)),
            *cast_specs,
        ],
        out_specs=[
            pl.BlockSpec((None, hpt, tm, HEAD_DIM), lambda i, j: (i // spt, j, i % spt, 0)),
            *cast_specs,
        ],
        out_shape=[jax.ShapeDtypeStruct((batch, ncols // HEAD_DIM, seq, HEAD_DIM), BF16), *cast_shapes],
        compiler_params=_params("parallel", "arbitrary"),
        name="proj_heads" if rope_table is None else "proj_rope_heads",
    )(xb, w, *tables, *(a for a, _ in w_casts))


def _moba_kernel(q_ref, k_ref, v_ref, ci_ref, o_ref, co_ref, vtg_ref, kmean_ref, bias_ref):
    blk, grp = MOBA_BLOCK, MOBA_GROUP
    quad = grp * blk
    hps, seq, _ = k_ref.shape
    nb = seq // blk
    ngrp = nb // grp
    nt = (((1,), (1,)), ((), ()))

    co_ref[...] = ci_ref[...].astype(co_ref.dtype)

    def setup(h):
        for j in range(nb):
            kj = k_ref[h, j * blk:(j + 1) * blk, :].astype(F32)
            kmean_ref[h, j:j + 1, :] = jnp.sum(kj, axis=0, keepdims=True) * (1.0 / blk)
            vt = v_ref[h, j * blk:(j + 1) * blk, :].astype(F32).T.astype(BF16)
            vtg_ref[h, j // grp, 0:HEAD_DIM, (j % grp) * blk:(j % grp + 1) * blk] = vt
        ones_row = lax.broadcasted_iota(jnp.int32, (vtg_ref.shape[2] - HEAD_DIM, quad), 0) == 0
        for g in range(ngrp):
            vtg_ref[h, g, HEAD_DIM:, :] = jnp.where(ones_row, 1.0, 0.0).astype(BF16)
        gate = lax.dot_general(kmean_ref[h].astype(BF16), q_ref[h], nt, preferred_element_type=F32)
        row = lax.broadcasted_iota(jnp.int32, gate.shape, 0)
        qblk = lax.shift_right_logical(lax.broadcasted_iota(jnp.int32, gate.shape, 1), blk.bit_length() - 1)
        past = row < qblk
        gate = jnp.where(past, gate, -jnp.inf)
        rank = jnp.zeros(gate.shape, jnp.int32)
        for jp in range(nb):
            gj = gate[jp:jp + 1, :]
            beats = (gj > gate) | ((gj == gate) & (jp < row))
            rank = rank + beats.astype(jnp.int32)
        bias = jnp.where(past & (rank < MOBA_TOPK), 0.0, -jnp.inf).astype(F32)
        for g in range(ngrp):
            for qd in range(g, ngrp):
                bias_ref[h, g, qd, 0:grp, :] = bias[g * grp:(g + 1) * grp, qd * quad:(qd + 1) * quad]

    for h in range(hps):
        setup(h)

    kpos = lax.broadcasted_iota(jnp.int32, (blk, blk), 0)
    qpos = lax.broadcasted_iota(jnp.int32, (blk, blk), 1)
    causal = kpos <= qpos

    def scores(k_rows, q):
        return lax.dot_general(k_rows, q, nt, preferred_element_type=F32)

    def tiles_max(tiles):
        m = jnp.max(tiles[0], axis=0, keepdims=True)
        for t in tiles[1:]:
            m = jnp.maximum(m, jnp.max(t, axis=0, keepdims=True))
        return m

    heads = range(hps)

    def diagonal_group(qd, q0):
        s = [[scores(k_ref[h, pl.ds(q0, (r + 1) * blk), :],
                     q_ref[h, pl.ds(pl.multiple_of(q0 + r * blk, blk), blk), :]) for r in range(grp)]
             for h in heads]
        m, p = [], []
        for h in heads:
            mr, pr = [], []
            for r in range(grp):
                brow = bias_ref[h, qd, qd, 0:grp, r * blk:(r + 1) * blk]
                tiles = [s[h][r][t * blk:(t + 1) * blk] + brow[t:t + 1, :] for t in range(r)]
                tiles.append(jnp.where(causal, s[h][r][r * blk:(r + 1) * blk], -jnp.inf))
                mr.append(tiles_max(tiles))
                pr.append(jnp.concatenate([jnp.exp2(t - mr[r]) for t in tiles], axis=0).astype(BF16))
            m.append(jnp.concatenate(mr, axis=1))
            p.append(pr)
        acc = [jnp.concatenate(
            [jnp.dot(vtg_ref[h, qd, :, 0:(r + 1) * blk], p[h][r], preferred_element_type=F32)
             for r in range(grp)], axis=1) for h in heads]
        return tuple((m[h], acc[h]) for h in heads)

    def group_scores(q0, g):
        k0 = pl.multiple_of(g * quad, quad)
        return [scores(k_ref[h, pl.ds(k0, quad), :], q_ref[h, pl.ds(q0, quad), :]) for h in heads]

    def past_group_recentred(qd, q0, g, state):
        s = group_scores(q0, g)
        stats, p = [], []
        for h in heads:
            m = state[h][0]
            brow = bias_ref[h, g, qd, 0:grp, :]
            tiles = [s[h][t * blk:(t + 1) * blk] + brow[t:t + 1, :] for t in range(grp)]
            m_new = jnp.maximum(m, tiles_max(tiles))
            p.append(jnp.concatenate([jnp.exp2(t - m_new) for t in tiles], axis=0).astype(BF16))
            stats.append((m_new, jnp.exp2(m - m_new)))
        pv = [jnp.dot(vtg_ref[h, g], p[h], preferred_element_type=F32) for h in heads]
        return tuple((stats[h][0], stats[h][1] * state[h][1] + pv[h]) for h in heads)

    def past_group(qd, q0, g, state):
        s = group_scores(q0, g)
        p, top = [], []
        for h in heads:
            shift = bias_ref[h, g, qd, 0:grp, :] - state[h][0]
            z = [s[h][t * blk:(t + 1) * blk] + shift[t:t + 1, :] for t in range(grp)]
            top.append(tiles_max(z))
            p.append(jnp.concatenate([jnp.exp2(t) for t in z], axis=0).astype(BF16))
        pv = [jnp.dot(vtg_ref[h, g], p[h], preferred_element_type=F32) for h in heads]
        worst = top[0]
        for h in heads[1:]:
            worst = jnp.maximum(worst, top[h])
        return lax.cond(
            jnp.max(worst) <= SOFTMAX_HEADROOM,
            lambda: tuple((state[h][0], state[h][1] + pv[h]) for h in heads),
            lambda: past_group_recentred(qd, q0, g, state))

    def query_quad(qd, _):
        q0 = pl.multiple_of(qd * quad, quad)
        state = lax.fori_loop(0, qd, functools.partial(past_group, qd, q0), diagonal_group(qd, q0))
        for h, (_, acc) in enumerate(state):
            out = acc[0:HEAD_DIM] / acc[HEAD_DIM:HEAD_DIM + 1]
            o_ref[pl.ds(q0, quad), h * HEAD_DIM:(h + 1) * HEAD_DIM] = out.T.astype(o_ref.dtype)
        return 0

    lax.fori_loop(0, ngrp, query_quad, 0)


def _moba_attention(qk, v, w_cast, *, seq, n_heads):
    batch = qk.shape[0]
    blk, grp = MOBA_BLOCK, MOBA_GROUP
    hps = _pick(n_heads, MOBA_HEADS_PER_STEP)
    nb = seq // blk
    assert seq % (blk * grp) == 0 and blk & (blk - 1) == 0 and grp <= V7X_SUBLANES_F32
    heads = lambda off: pl.BlockSpec((None, hps, seq, HEAD_DIM), lambda b, h: (b, h + off, 0, 0))
    grid = (batch, n_heads // hps)
    cast_spec, cast_shape = _cast_stream(w_cast, grid)
    return pl.pallas_call(
        _moba_kernel,
        grid=grid,
        in_specs=[heads(0), heads(n_heads // hps), heads(0), cast_spec],
        out_specs=[pl.BlockSpec((None, seq, hps * HEAD_DIM), lambda b, h: (b, 0, h)), cast_spec],
        out_shape=[jax.ShapeDtypeStruct((batch, seq, n_heads * HEAD_DIM), BF16), cast_shape],
        scratch_shapes=[
            pltpu.VMEM((hps, nb // grp, HEAD_DIM + V7X_SUBLANES_BF16, grp * blk), BF16),
            pltpu.VMEM((hps, nb, HEAD_DIM), F32),
            pltpu.VMEM((hps, nb // grp, nb // grp, V7X_SUBLANES_F32, grp * blk), F32),
        ],
        compiler_params=_params("parallel", "parallel"),
        name="moba_attention",
    )(qk, qk, v, w_cast)


def _layer_norm_tiles(tile, nj, tn, gb_ref, o_ref, ob_ref=None):
    d = nj * tn
    tot = jnp.sum(tile(0), axis=1, keepdims=True)
    for jj in range(1, nj):
        tot = tot + jnp.sum(tile(jj), axis=1, keepdims=True)
    mu = tot * (1.0 / d)
    sq = jnp.sum(jnp.square(tile(0) - mu), axis=1, keepdims=True)
    for jj in range(1, nj):
        sq = sq + jnp.sum(jnp.square(tile(jj) - mu), axis=1, keepdims=True)
    rstd = lax.rsqrt(sq * (1.0 / d) + LN_EPS)
    for jj in range(nj):
        cs = slice(jj * tn, (jj + 1) * tn)
        y = (tile(jj) - mu) * rstd * gb_ref[0:1, cs] + gb_ref[1:2, cs]
        o_ref[:, cs] = y
        if ob_ref is not None:
            ob_ref[:, cs] = y.astype(ob_ref.dtype)


def _outproj_ln_kernel(a_ref, p_ref, w_ref, x_ref, gb_ref, o_ref, ob_ref, *, alpha, tn):
    ka = a_ref.shape[1]
    nj = o_ref.shape[1] // tn
    for jj in range(nj):
        cs = slice(jj * tn, (jj + 1) * tn)
        mix = jnp.dot(a_ref[...], w_ref[0:ka, cs], preferred_element_type=F32)
        mix = mix + jnp.dot(p_ref[...], w_ref[ka:, cs], preferred_element_type=F32)
        o_ref[:, cs] = alpha * x_ref[:, cs] + mix
    _layer_norm_tiles(lambda jj: o_ref[:, jj * tn:(jj + 1) * tn], nj, tn, gb_ref, o_ref, ob_ref)


def _outproj_ln(attn, pool, w_out, x2, gb, *, alpha):
    T, D = x2.shape
    ka, kp = attn.shape[1], pool.shape[1]
    tm = _pick(T, 256)
    return pl.pallas_call(
        functools.partial(_outproj_ln_kernel, alpha=alpha, tn=_pick(D, OUTPROJ_COL_TILE)),
        grid=(T // tm,),
        in_specs=[
            pl.BlockSpec((tm, ka), lambda i: (i, 0)),
            pl.BlockSpec((tm, kp), lambda i: (i, 0)),
            pl.BlockSpec((ka + kp, D), lambda i: (0, 0), pipeline_mode=pl.Buffered(1)),
            pl.BlockSpec((tm, D), lambda i: (i, 0)),
            pl.BlockSpec((2, D), lambda i: (0, 0)),
        ],
        out_specs=[pl.BlockSpec((tm, D), lambda i: (i, 0))] * 2,
        out_shape=[jax.ShapeDtypeStruct((T, D), F32), jax.ShapeDtypeStruct((T, D), BF16)],
        compiler_params=_params("parallel"),
        name="outproj_ln1",
    )(attn, pool, w_out, x2, gb)


def _ffn_up_kernel(x_ref, w_ref, o_ref):
    z = jnp.dot(x_ref[...], w_ref[...], preferred_element_type=F32)
    o_ref[...] = jnp.square(jnp.maximum(z, 0.0)).astype(o_ref.dtype)


def _ffn_up(hb, w1):
    T, D = hb.shape
    F = w1.shape[1]
    tm = _pick(T, 1024)
    tn = _pick(F, 1024)
    return pl.pallas_call(
        _ffn_up_kernel,
        grid=(T // tm, F // tn),
        in_specs=[pl.BlockSpec((tm, D), lambda i, j: (i, 0)), pl.BlockSpec((D, tn), lambda i, j: (0, j))],
        out_specs=pl.BlockSpec((tm, tn), lambda i, j: (i, j)),
        out_shape=jax.ShapeDtypeStruct((T, F), BF16),
        compiler_params=_params("parallel", "arbitrary"),
        name="ffn_up",
    )(hb, w1)


def _ffn_ln_kernel(hid_ref, h_hbm, w2_ref, gb_ref, o_ref, res_ref, res_sem, *, alpha, tn):
    f = pl.program_id(1)
    tm = o_ref.shape[0]
    nj = o_ref.shape[1] // tn
    rows = pl.ds(pl.multiple_of(pl.program_id(0) * tm, tm), tm)
    residual_copy = pltpu.make_async_copy(h_hbm.at[rows, :], res_ref, res_sem)

    @pl.when(f == 0)
    def _():
        residual_copy.start()
        o_ref[...] = jnp.zeros_like(o_ref)

    hid = hid_ref[...]
    for jj in range(nj):
        cs = slice(jj * tn, (jj + 1) * tn)
        o_ref[:, cs] += jnp.dot(hid, w2_ref[:, cs], preferred_element_type=F32)

    @pl.when(f == pl.num_programs(1) - 1)
    def _():
        residual_copy.wait()
        o_ref[...] = alpha * res_ref[...] + o_ref[...]
        _layer_norm_tiles(lambda jj: o_ref[:, jj * tn:(jj + 1) * tn], nj, tn, gb_ref, o_ref)


def _ffn_ln(hid, h, w2, gb, *, alpha):
    T, D = h.shape
    F = hid.shape[1]
    tm = _pick(T, 512)
    tf = _pick(F, FFN_HIDDEN_TILE)
    tn = _pick(D, 512)
    return pl.pallas_call(
        functools.partial(_ffn_ln_kernel, alpha=alpha, tn=tn),
        grid=(T // tm, F // tf),
        in_specs=[
            pl.BlockSpec((tm, tf), lambda i, f: (i, f)),
            pl.BlockSpec(memory_space=pl.ANY),
            pl.BlockSpec((tf, D), lambda i, f: (f, 0)),
            pl.BlockSpec((2, D), lambda i, f: (0, 0)),
        ],
        out_specs=pl.BlockSpec((tm, D), lambda i, f: (i, 0)),
        out_shape=jax.ShapeDtypeStruct((T, D), F32),
        scratch_shapes=[pltpu.VMEM((tm, D), F32), pltpu.SemaphoreType.DMA(())],
        compiler_params=_params("parallel", "arbitrary"),
        name="ffn_ln2",
    )(hid, h, w2, gb)


def _rope_table(seq, q_scale):
    inv_freq = 1.0 / (ROPE_THETA ** (jnp.arange(0, HEAD_DIM, 2, dtype=F32) / HEAD_DIM))
    ang = jnp.arange(seq, dtype=F32)[:, None] * inv_freq[None, :]
    cos, sin = jnp.cos(ang), jnp.sin(ang)
    cos = jnp.concatenate([cos, cos], axis=-1)
    sin = jnp.concatenate([-sin, sin], axis=-1)
    return jnp.stack([jnp.stack([q_scale * cos, q_scale * sin]), jnp.stack([cos, sin])])


def kernel(x, w_in, w_pool, pool_scale, w_out, ln1_g, ln1_b, w_ff1, w_ff2, ln2_g, ln2_b):
    B, S, D = x.shape
    depth = w_in.shape[0]
    attn_w = D // 2
    n_heads = attn_w // HEAD_DIM
    pool_w = D - attn_w
    alpha = (2.0 * depth) ** 0.25
    rope_table = _rope_table(S, HEAD_DIM ** -0.5 * LOG2_E)

    x2 = x.reshape(B * S, D)
    for l in range(depth):
        w_vu_b = w_in[l][:, 2 * attn_w:].astype(BF16)
        pool, xb = _project_pool(x2, w_vu_b, w_pool[l].astype(BF16), pool_scale[l].reshape(1, pool_w),
                                 seq=S, col0=attn_w)
        v, w_out_b, w_qk_b = _project_heads(xb, w_vu_b, [(w_out[l], None), (w_in[l], 2 * attn_w)],
                                            batch=B, seq=S, col0=0, ncols=attn_w)
        qk, w_ff1_b = _project_heads(xb, w_qk_b, [(w_ff1[l], None)], batch=B, seq=S, col0=0,
                                     ncols=2 * attn_w, rope_table=rope_table)

        attn, w_ff2_b = _moba_attention(qk, v, w_ff2[l], seq=S, n_heads=n_heads)
        h, hb = _outproj_ln(attn.reshape(B * S, attn_w), pool, w_out_b, x2,
                            jnp.stack([ln1_g[l], ln1_b[l]]), alpha=alpha)
        x2 = _ffn_ln(_ffn_up(hb, w_ff1_b), h, w_ff2_b, jnp.stack([ln2_g[l], ln2_b[l]]), alpha=alpha)
    return x2.reshape(B, S, D)
```
